```python
import math
import jax, jax.numpy as jnp
from jax import lax
import numpy as np

D_MODEL = 2048
BATCH = 2
SEQ = 4096
DEPTH = 2
DEC_BATCH = 128
DEC_SEQ = 8
PAST_LEN = 2048
PAGE_SIZE = 128

HG_HEADS = 8
HG_KDIM = 128
HG_VDIM = 128
HG_QK = HG_HEADS * HG_KDIM
HG_WIDTH = HG_HEADS * HG_VDIM
HG_CHUNK = 64
FOX_HEADS = 8
FOX_HDIM = 128
FOX_WIDTH = FOX_HEADS * FOX_HDIM
Q_BLOCK = 128
MIX_WIDTH = HG_WIDTH + FOX_WIDTH
D_FF = -(-8 * D_MODEL // (3 * 256)) * 256
NORM_EPS = 1e-6
FOX_FORGET_BIAS = 2.0
IN_SIZES = (HG_QK, HG_QK, HG_WIDTH, HG_WIDTH, FOX_WIDTH, FOX_WIDTH, FOX_WIDTH, FOX_HEADS)
IN_DIM = sum(IN_SIZES)
IN_SPLITS = tuple(int(s) for s in np.cumsum(IN_SIZES)[:-1])

kernel_name = "hymba_hgrn2_fox_decoder_step"


def rmsnorm(x, w):
    x32 = x.astype(jnp.float32)
    y = x32 * lax.rsqrt(jnp.mean(x32 * x32, axis=-1, keepdims=True) + NORM_EPS)
    return (y * w.astype(jnp.float32)).astype(x.dtype)


def hgrn2_mix(q_raw, f_raw, i_raw, g_raw, lb, gn_w, S0):
    B, T, _ = q_raw.shape
    f32 = jnp.float32

    def heads(a, d):
        return a.reshape(B, T, HG_HEADS, d).transpose(0, 2, 1, 3).astype(f32)

    q = heads(jax.nn.silu(q_raw), HG_KDIM)
    z = heads(f_raw, HG_KDIM)
    lbh = lb.astype(f32).reshape(1, HG_HEADS, 1, HG_KDIM)
    logf = jnp.logaddexp(jnp.log(lbh), jnp.log1p(-lbh) + jax.nn.log_sigmoid(z))
    k = (1.0 - lbh) * jax.nn.sigmoid(-z)
    v = heads(i_raw, HG_VDIM)
    C = math.gcd(T, HG_CHUNK)
    n = T // C

    def chunks(a):
        return a.reshape(B, HG_HEADS, n, C, a.shape[-1]).transpose(2, 0, 1, 3, 4)

    causal = jnp.tril(jnp.ones((C, C), dtype=bool))[None, None, :, :, None]

    def step(S, inp):
        qc, kc, vc, gc = inp
        b = jnp.cumsum(gc, axis=2)
        rel = jnp.where(causal, b[:, :, :, None, :] - b[:, :, None, :, :], -jnp.inf)
        att = jnp.einsum('bhtk,bhsk,bhtsk->bhts', qc, kc, jnp.exp(rel))
        o = (jnp.einsum('bhtk,bhkv->bhtv', qc * jnp.exp(b), S)
             + jnp.einsum('bhts,bhsv->bhtv', att, vc))
        b_end = b[:, :, -1:, :]
        S = (jnp.exp(b_end[:, :, 0, :])[..., None] * S
             + jnp.einsum('bhsk,bhsv->bhkv', kc * jnp.exp(b_end - b), vc))
        return S, o

    S_fin, o = lax.scan(step, S0.astype(f32), (chunks(q), chunks(k), chunks(v), chunks(logf)))
    o = o.transpose(1, 2, 0, 3, 4).reshape(B, HG_HEADS, T, HG_VDIM)
    o = o * lax.rsqrt(jnp.mean(o * o, axis=-1, keepdims=True) + NORM_EPS) * gn_w.astype(f32)
    o = o.transpose(0, 2, 1, 3).reshape(B, T, HG_WIDTH) * jax.nn.silu(g_raw.astype(f32))
    return o.astype(q_raw.dtype), S_fin


def fox_prompt(q, k, v, logf):
    B, T, H, D = q.shape
    scale = D ** -0.5
    c = jnp.cumsum(logf, axis=1).transpose(0, 2, 1)
    nb = T // Q_BLOCK
    qb = q.reshape(B, nb, Q_BLOCK, H, D).transpose(1, 0, 2, 3, 4)
    cb = c.reshape(B, H, nb, Q_BLOCK).transpose(2, 0, 1, 3)
    kpos = jnp.arange(T)

    def block(args):
        qi, ci, j = args
        qpos = j * Q_BLOCK + jnp.arange(Q_BLOCK)
        s = (jnp.einsum('bqhd,bkhd->bhqk', qi, k).astype(jnp.float32) * scale
             + (ci[..., None] - c[:, :, None, :]))
        s = jnp.where(kpos[None, :] <= qpos[:, None], s, -jnp.inf)
        p = jax.nn.softmax(s, axis=-1).astype(v.dtype)
        return jnp.einsum('bhqk,bkhd->bqhd', p, v)

    o = lax.map(block, (qb, cb, jnp.arange(nb)))
    return o.transpose(1, 0, 2, 3, 4).reshape(B, T, H * D)


def fox_sample(q, k, v, logf, k_past, v_past, logf_past):
    Bd, T, H, D = q.shape
    P = k_past.shape[1]
    scale = D ** -0.5
    c_past = jnp.cumsum(logf_past.astype(jnp.float32), axis=1)
    c_new = c_past[:, -1:, :] + jnp.cumsum(logf.astype(jnp.float32), axis=1)
    cpT = c_past.transpose(0, 2, 1)
    cnT = c_new.transpose(0, 2, 1)
    s_past = (jnp.einsum('bqhd,bkhd->bhqk', q, k_past).astype(jnp.float32) * scale
              + (cnT[..., None] - cpT[:, :, None, :]))
    s_new = (jnp.einsum('bqhd,bkhd->bhqk', q, k).astype(jnp.float32) * scale
             + (cnT[..., None] - cnT[:, :, None, :]))
    s_new = jnp.where(jnp.tril(jnp.ones((T, T), dtype=bool)), s_new, -jnp.inf)
    p = jax.nn.softmax(jnp.concatenate([s_past, s_new], axis=-1), axis=-1).astype(v.dtype)
    o = (jnp.einsum('bhqk,bkhd->bqhd', p[..., :P], v_past)
         + jnp.einsum('bhqk,bkhd->bqhd', p[..., P:], v))
    return o.reshape(Bd, T, H * D)


def token_mix(h, w_in_l, b_f_l, lb_l, gn_w_l, w_out_l, S0, past):
    B, T, _ = h.shape
    z = jnp.einsum('btd,de->bte', h, w_in_l)
    hq, hf, hi, hg, fq, fk, fv, ff = jnp.split(z, IN_SPLITS, axis=-1)
    o_hg, S_fin = hgrn2_mix(hq, hf, hi, hg, lb_l, gn_w_l, S0)
    fq = fq.reshape(B, T, FOX_HEADS, FOX_HDIM)
    fk = fk.reshape(B, T, FOX_HEADS, FOX_HDIM)
    fv = fv.reshape(B, T, FOX_HEADS, FOX_HDIM)
    logf = jax.nn.log_sigmoid((ff + b_f_l).astype(jnp.float32))
    if past is None:
        o_fx = fox_prompt(fq, fk, fv, logf)
    else:
        o_fx = fox_sample(fq, fk, fv, logf, past[0], past[1], past[2])
    o = jnp.concatenate([o_hg, o_fx.astype(o_hg.dtype)], axis=-1)
    y = jnp.einsum('bte,ed->btd', o, w_out_l)
    return y, fk, fv, logf, S_fin


def swiglu(h, wg, wu, wd):
    a = jnp.einsum('btd,df->btf', h, wg)
    b = jnp.einsum('btd,df->btf', h, wu)
    return jnp.einsum('btf,fd->btd', jax.nn.silu(a) * b, wd)


def setup_inputs(seed: int = 0) -> dict:
    key = jax.random.key(seed)
    ks = jax.random.split(key, 24)
    f32 = jnp.float32
    n_pages = PAST_LEN // PAGE_SIZE
    n_used = DEC_BATCH * n_pages
    n_pool = (5 * n_used + 3) // 4
    nrm = lambda k, shp: jax.random.normal(k, shp, f32)
    x_prompt = nrm(ks[0], (BATCH, SEQ, D_MODEL))
    x_sample = nrm(ks[1], (DEC_BATCH, DEC_SEQ, D_MODEL))
    cache_k = nrm(ks[2], (DEPTH, n_pool, PAGE_SIZE, FOX_HEADS, FOX_HDIM))
    cache_v = nrm(ks[3], (DEPTH, n_pool, PAGE_SIZE, FOX_HEADS, FOX_HDIM))
    cache_logf = jax.nn.log_sigmoid(FOX_FORGET_BIAS + 0.5 * nrm(ks[4], (DEPTH, n_pool, PAGE_SIZE, FOX_HEADS)))
    state_hgrn = 0.5 * nrm(ks[5], (DEPTH, DEC_BATCH, HG_HEADS, HG_KDIM, HG_VDIM))
    page_table = jax.random.permutation(ks[6], n_pool)[:n_used].reshape(DEC_BATCH, n_pages).astype(jnp.int32)
    w_in = nrm(ks[7], (DEPTH, D_MODEL, IN_DIM)) * D_MODEL ** -0.5
    b_fox_f = FOX_FORGET_BIAS + 0.5 * nrm(ks[8], (DEPTH, FOX_HEADS))
    hg_lower_bounds = nrm(ks[9], (DEPTH, HG_QK))
    hg_norm_w = 1.0 + 0.02 * nrm(ks[10], (DEPTH, HG_VDIM))
    w_out = nrm(ks[11], (DEPTH, MIX_WIDTH, D_MODEL)) * MIX_WIDTH ** -0.5
    norm_mix_pre = 1.0 + 0.02 * nrm(ks[12], (DEPTH, D_MODEL))
    norm_mix_post = 1.0 + 0.02 * nrm(ks[13], (DEPTH, D_MODEL))
    norm_ffn_pre = 1.0 + 0.02 * nrm(ks[14], (DEPTH, D_MODEL))
    norm_ffn_post = 1.0 + 0.02 * nrm(ks[15], (DEPTH, D_MODEL))
    w_gate = nrm(ks[16], (DEPTH, D_MODEL, D_FF)) * D_MODEL ** -0.5
    w_up = nrm(ks[17], (DEPTH, D_MODEL, D_FF)) * D_MODEL ** -0.5
    w_down = nrm(ks[18], (DEPTH, D_FF, D_MODEL)) * D_FF ** -0.5
    return {"x_prompt": x_prompt, "x_sample": x_sample, "cache_k": cache_k, "cache_v": cache_v,
            "cache_logf": cache_logf, "state_hgrn": state_hgrn, "page_table": page_table,
            "w_in": w_in, "b_fox_f": b_fox_f, "hg_lower_bounds": hg_lower_bounds, "hg_norm_w": hg_norm_w,
            "w_out": w_out, "norm_mix_pre": norm_mix_pre, "norm_mix_post": norm_mix_post,
            "norm_ffn_pre": norm_ffn_pre, "norm_ffn_post": norm_ffn_post,
            "w_gate": w_gate, "w_up": w_up, "w_down": w_down}


def reference(x_prompt, x_sample, cache_k, cache_v, cache_logf, state_hgrn, page_table,
              w_in, b_fox_f, hg_lower_bounds, hg_norm_w, w_out,
              norm_mix_pre, norm_mix_post, norm_ffn_pre, norm_ffn_post,
              w_gate, w_up, w_down):
    lbs = jnp.cumsum(jax.nn.softmax(hg_lower_bounds.astype(jnp.float32), axis=0), axis=0)
    lbs = lbs - lbs[0:1]
    Bd, n_pages = page_table.shape
    past_len = n_pages * PAGE_SIZE
    xp, xs = x_prompt, x_sample
    kp, vp, gp, Sp, ksm, vsm, gsm, Ssm = [], [], [], [], [], [], [], []
    for l in range(DEPTH):
        h = rmsnorm(xp, norm_mix_pre[l])
        S0 = jnp.zeros((xp.shape[0], HG_HEADS, HG_KDIM, HG_VDIM), jnp.float32)
        y, k_new, v_new, g_new, S_fin = token_mix(h, w_in[l], b_fox_f[l], lbs[l], hg_norm_w[l], w_out[l], S0, None)
        xp = xp + rmsnorm(y, norm_mix_post[l])
        xp = xp + rmsnorm(swiglu(rmsnorm(xp, norm_ffn_pre[l]), w_gate[l], w_up[l], w_down[l]), norm_ffn_post[l])
        kp.append(k_new); vp.append(v_new); gp.append(g_new); Sp.append(S_fin)
        k_past = cache_k[l, page_table].reshape(Bd, past_len, FOX_HEADS, FOX_HDIM)
        v_past = cache_v[l, page_table].reshape(Bd, past_len, FOX_HEADS, FOX_HDIM)
        g_past = cache_logf[l, page_table].reshape(Bd, past_len, FOX_HEADS)
        h = rmsnorm(xs, norm_mix_pre[l])
        y, k_new, v_new, g_new, S_fin = token_mix(h, w_in[l], b_fox_f[l], lbs[l], hg_norm_w[l], w_out[l],
                                                  state_hgrn[l], (k_past, v_past, g_past))
        xs = xs + rmsnorm(y, norm_mix_post[l])
        xs = xs + rmsnorm(swiglu(rmsnorm(xs, norm_ffn_pre[l]), w_gate[l], w_up[l], w_down[l]), norm_ffn_post[l])
        ksm.append(k_new); vsm.append(v_new); gsm.append(g_new); Ssm.append(S_fin)
    k_prompt = jnp.stack(kp)
    v_prompt = jnp.stack(vp)
    logf_prompt = jnp.stack(gp)
    hgrn_prompt = jnp.stack(Sp)
    k_sample = jnp.stack(ksm)
    v_sample = jnp.stack(vsm)
    logf_sample = jnp.stack(gsm)
    hgrn_sample = jnp.stack(Ssm)
    return (xp, xs, k_prompt, v_prompt, logf_prompt, hgrn_prompt, k_sample, v_sample, logf_sample, hgrn_sample)
```

```python
import functools

import jax
import jax.numpy as jnp
from jax import lax
from jax.experimental import pallas as pl
from jax.experimental.pallas import tpu as pltpu

NORM_EPS = 1e-6
LANES = 128
SUBLANES = 8
VMEM_LIMIT_BYTES = 56 * 1024 * 1024

F32 = jnp.float32
BF16 = jnp.bfloat16

_NT = (((1,), (1,)), ((), ()))
_TN = (((0,), (0,)), ((), ()))


def _params(*sem):
    return pltpu.CompilerParams(dimension_semantics=sem, vmem_limit_bytes=VMEM_LIMIT_BYTES)


def _rms(x):
    return x * lax.rsqrt(jnp.mean(x * x, axis=-1, keepdims=True) + NORM_EPS)


def _sigmoid_parts(z):
    e = jnp.exp(-jnp.abs(z))
    r = 1.0 / (1.0 + e)
    pos = z >= 0
    return jnp.where(pos, r, e * r), jnp.where(pos, e * r, r), jnp.minimum(z, 0.0) - jnp.log1p(e)


def _cumsum_rows(x, period=None):
    n = x.shape[0]
    span = n if period is None else period
    row = lax.broadcasted_iota(jnp.int32, x.shape, 0)
    if period is not None:
        row = row % period
    s = 1
    while s < span:
        x = x + jnp.where(row >= s, pltpu.roll(x, s, axis=0), 0.0)
        s *= 2
    return x


def _cumsum_lanes(x):
    n = x.shape[-1]
    lane = lax.broadcasted_iota(jnp.int32, x.shape, x.ndim - 1)
    s = 1
    while s < n:
        x = x + jnp.where(lane >= s, pltpu.roll(x, s, axis=x.ndim - 1), 0.0)
        s *= 2
    return x


def _inproj_kernel(x_ref, nw_ref, w_ref, wff_ref, bf_ref, z_ref, logf_ref, c_ref,
                   h_scr, carry_scr, *, seq_tiles):
    i = pl.program_id(0)
    j = pl.program_id(1)

    @pl.when(j == 0)
    def _():
        hb = (_rms(x_ref[...]) * nw_ref[...]).astype(BF16)
        h_scr[...] = hb
        ff = jnp.dot(hb, wff_ref[...], preferred_element_type=F32) + bf_ref[...]
        _, _, lf = _sigmoid_parts(ff)
        logf_ref[...] = lf

        @pl.when(i % seq_tiles == 0)
        def _():
            carry_scr[...] = jnp.zeros_like(carry_scr)

        c = _cumsum_rows(lf) + carry_scr[...]
        c_ref[...] = c
        carry_scr[...] = c[c.shape[0] - 1:, :]

    z_ref[...] = jnp.dot(h_scr[...], w_ref[...], preferred_element_type=F32)


def _inproj(x, nw, w_main, w_ff, b_ff, *, tm, tn, seq_len):
    m, d = x.shape
    n = w_main.shape[1]
    kern = functools.partial(_inproj_kernel, seq_tiles=seq_len // tm)
    return pl.pallas_call(
        kern,
        out_shape=(jax.ShapeDtypeStruct((m, n), F32),
                   jax.ShapeDtypeStruct((m, LANES), F32),
                   jax.ShapeDtypeStruct((m, LANES), F32)),
        grid=(m // tm, n // tn),
        in_specs=[pl.BlockSpec((tm, d), lambda i, j: (i, 0)),
                  pl.BlockSpec((1, d), lambda i, j: (0, 0)),
                  pl.BlockSpec((d, tn), lambda i, j: (0, j)),
                  pl.BlockSpec((d, LANES), lambda i, j: (0, 0)),
                  pl.BlockSpec((1, LANES), lambda i, j: (0, 0))],
        out_specs=(pl.BlockSpec((tm, tn), lambda i, j: (i, j)),
                   pl.BlockSpec((tm, LANES), lambda i, j: (i, 0)),
                   pl.BlockSpec((tm, LANES), lambda i, j: (i, 0))),
        scratch_shapes=[pltpu.VMEM((tm, d), BF16), pltpu.VMEM((1, LANES), F32)],
        compiler_params=_params("arbitrary", "arbitrary"),
        name="inproj",
    )(x, nw, w_main, w_ff, b_ff)


def _lower_bound(raw, layer):
    depth = raw.shape[0]
    rows = [raw[i:i + 1, :] for i in range(depth)]
    mx = functools.reduce(jnp.maximum, rows)
    ex = [jnp.exp(r - mx) for r in rows]
    tot = functools.reduce(lambda a, b: a + b, ex)
    p = [e / tot for e in ex]
    cum = functools.reduce(lambda a, b: a + b, p[:layer + 1])
    return cum - p[0]


def _hgrn_gates(lb, qr, z, chunk):
    sig_q, _, _ = _sigmoid_parts(qr)
    q = qr * sig_q
    _, sig_nz, logsig = _sigmoid_parts(z)
    a = jnp.log(lb)
    bb = jnp.log1p(-lb) + logsig
    logf = jnp.maximum(a, bb) + jnp.log1p(jnp.exp(-jnp.abs(a - bb)))
    k = (1.0 - lb) * sig_nz
    return q, k, _cumsum_rows(logf, period=chunk)


def _hgrn_chunk(q, k, v, b, st):
    c = q.shape[0]
    ref = b[c // 2 - 1:c // 2, :]
    end = b[c - 1:c, :]
    vb = v.astype(BF16)
    qi = (q * jnp.exp(b - ref)).astype(BF16)
    ki = (k * jnp.exp(ref - b)).astype(BF16)
    att = lax.dot_general(qi, ki, _NT, preferred_element_type=F32)
    tri = (lax.broadcasted_iota(jnp.int32, (c, c), 0) >= lax.broadcasted_iota(jnp.int32, (c, c), 1))
    att = jnp.where(tri, att, 0.0).astype(BF16)
    qs = (q * jnp.exp(b)).astype(BF16)
    o = (lax.dot_general(qs, st.astype(BF16), _NT, preferred_element_type=F32)
         + jnp.dot(att, vb, preferred_element_type=F32))
    ks = (k * jnp.exp(end - b)).astype(BF16)
    st = st * jnp.exp(end) + lax.dot_general(vb, ks, _TN, preferred_element_type=F32)
    return o, st


def _hgrn_finish(o, g, gn):
    sig_g, _, _ = _sigmoid_parts(g)
    return (_rms(o) * gn * (g * sig_g)).astype(BF16)


def _hgrn_prompt_kernel(lb_ref, q_ref, f_ref, i_ref, g_ref, gn_ref, o_ref, sfin_ref,
                        st_scr, o_scr, *, chunk, layer):
    t = pl.program_id(2)

    @pl.when(t == 0)
    def _():
        st_scr[...] = jnp.zeros_like(st_scr)

    lb = _lower_bound(lb_ref[...], layer)
    q, k, b = _hgrn_gates(lb, q_ref[...], f_ref[...], chunk)
    v = i_ref[...]
    st = st_scr[...]
    for c in range(q.shape[0] // chunk):
        sl = slice(c * chunk, (c + 1) * chunk)
        o, st = _hgrn_chunk(q[sl], k[sl], v[sl], b[sl], st)
        o_scr[sl, :] = o
    st_scr[...] = st
    o_ref[...] = _hgrn_finish(o_scr[...], g_ref[...], gn_ref[...])

    @pl.when(t == pl.num_programs(2) - 1)
    def _():
        sfin_ref[...] = st.T


def _hgrn_prompt(z, lb_raw, gn, *, batch, seq, heads, layer, chunk, tt):
    nt = seq // tt
    col = lambda base: (lambda b, h, t: (b * nt + t, base + h))
    kern = functools.partial(_hgrn_prompt_kernel, chunk=chunk, layer=layer)
    depth = lb_raw.shape[0]
    return pl.pallas_call(
        kern,
        out_shape=(jax.ShapeDtypeStruct((batch * seq, heads * LANES), BF16),
                   jax.ShapeDtypeStruct((batch, heads, LANES, LANES), F32)),
        grid=(batch, heads, nt),
        in_specs=[pl.BlockSpec((depth, LANES), lambda b, h, t: (0, h)),
                  pl.BlockSpec((tt, LANES), col(0)),
                  pl.BlockSpec((tt, LANES), col(heads)),
                  pl.BlockSpec((tt, LANES), col(2 * heads)),
                  pl.BlockSpec((tt, LANES), col(3 * heads)),
                  pl.BlockSpec((1, LANES), lambda b, h, t: (0, 0))],
        out_specs=(pl.BlockSpec((tt, LANES), lambda b, h, t: (b * nt + t, h)),
                   pl.BlockSpec((None, None, LANES, LANES), lambda b, h, t: (b, h, 0, 0))),
        scratch_shapes=[pltpu.VMEM((LANES, LANES), F32), pltpu.VMEM((tt, LANES), F32)],
        compiler_params=_params("arbitrary", "arbitrary", "arbitrary"),
        name="hgrn_prompt",
    )(lb_raw, z, z, z, z, gn)


def _hgrn_sample_kernel(lb_ref, q_ref, f_ref, i_ref, g_ref, gn_ref, s0_ref, o_ref, sfin_ref,
                        o_scr, *, chunk, layer):
    lb = _lower_bound(lb_ref[...], layer)
    q, k, b = _hgrn_gates(lb, q_ref[...], f_ref[...], chunk)
    v = i_ref[...]
    for s in range(q.shape[0] // chunk):
        sl = slice(s * chunk, (s + 1) * chunk)
        o, st = _hgrn_chunk(q[sl], k[sl], v[sl], b[sl], s0_ref[s].T)
        o_scr[sl, :] = o
        sfin_ref[s] = st.T
    o_ref[...] = _hgrn_finish(o_scr[...], g_ref[...], gn_ref[...])


def _hgrn_sample(z, lb_raw, gn, state, *, row0, nseq, steps, heads, layer, bb):
    rows = bb * steps
    rb0 = row0 // rows
    col = lambda base: (lambda i, h: (rb0 + i, base + h))
    kern = functools.partial(_hgrn_sample_kernel, chunk=steps, layer=layer)
    depth = lb_raw.shape[0]
    return pl.pallas_call(
        kern,
        out_shape=(jax.ShapeDtypeStruct((nseq * steps, heads * LANES), BF16),
                   jax.ShapeDtypeStruct((nseq, heads, LANES, LANES), F32)),
        grid=(nseq // bb, heads),
        in_specs=[pl.BlockSpec((depth, LANES), lambda i, h: (0, h)),
                  pl.BlockSpec((rows, LANES), col(0)),
                  pl.BlockSpec((rows, LANES), col(heads)),
                  pl.BlockSpec((rows, LANES), col(2 * heads)),
                  pl.BlockSpec((rows, LANES), col(3 * heads)),
                  pl.BlockSpec((1, LANES), lambda i, h: (0, 0)),
                  pl.BlockSpec((None, bb, None, LANES, LANES), lambda i, h: (layer, i, h, 0, 0))],
        out_specs=(pl.BlockSpec((rows, LANES), lambda i, h: (i, h)),
                   pl.BlockSpec((bb, None, LANES, LANES), lambda i, h: (i, h, 0, 0))),
        scratch_shapes=[pltpu.VMEM((rows, LANES), F32)],
        compiler_params=_params("arbitrary", "arbitrary"),
        name="hgrn_sample",
    )(lb_raw, z, z, z, z, gn, state)


def _softmax_step(s, v, m, l, acc):
    m_new = jnp.maximum(m, jnp.max(s, axis=-1, keepdims=True))
    alpha = jnp.exp(m - m_new)
    p = jnp.exp(s - m_new)
    l = alpha * l + jnp.sum(p, axis=-1, keepdims=True)
    acc = alpha * acc + jnp.dot(p.astype(BF16), v, preferred_element_type=F32)
    return m_new, l, acc


def _fox_prompt_kernel(q_ref, k_ref, v_ref, ck_ref, o_ref, *, tq, scale):
    qi = pl.program_id(2)
    q = (q_ref[...] * scale).astype(BF16)

    def scores(j):
        start = pl.multiple_of(j * tq, tq)
        kj = k_ref[pl.ds(start, tq), :].astype(BF16)
        vj = v_ref[pl.ds(start, tq), :].astype(BF16)
        s = lax.dot_general(q, kj, _NT, preferred_element_type=F32)
        return s - ck_ref[:, pl.ds(start, tq)], vj

    def body(j, carry):
        s, vj = scores(j)
        return _softmax_step(s, vj, *carry)

    init = (jnp.full((tq, 1), -jnp.inf, F32), jnp.zeros((tq, 1), F32), jnp.zeros((tq, LANES), F32))
    carry = lax.fori_loop(0, qi, body, init)
    s, vj = scores(qi)
    causal = (lax.broadcasted_iota(jnp.int32, (tq, tq), 0) >= lax.broadcasted_iota(jnp.int32, (tq, tq), 1))
    _, l, acc = _softmax_step(jnp.where(causal, s, -jnp.inf), vj, *carry)
    o_ref[...] = (acc / l).astype(BF16)


def _fox_prompt(z, c_t, *, batch, seq, heads, tq):
    nq = seq // tq
    d = LANES
    qcol, kcol, vcol = 4 * heads, 5 * heads, 6 * heads
    kern = functools.partial(_fox_prompt_kernel, tq=tq, scale=d ** -0.5)
    return pl.pallas_call(
        kern,
        out_shape=jax.ShapeDtypeStruct((batch * seq, heads * d), BF16),
        grid=(batch, heads, nq),
        in_specs=[pl.BlockSpec((tq, d), lambda b, h, i: (b * nq + i, qcol + h)),
                  pl.BlockSpec((seq, d), lambda b, h, i: (b, kcol + h)),
                  pl.BlockSpec((seq, d), lambda b, h, i: (b, vcol + h)),
                  pl.BlockSpec((None, None, 1, seq), lambda b, h, i: (b, h, 0, 0))],
        out_specs=pl.BlockSpec((tq, d), lambda b, h, i: (b * nq + i, h)),
        compiler_params=_params("arbitrary", "arbitrary", "arbitrary"),
        name="fox_prompt",
    )(z, z, z, c_t)


def _fox_decode_kernel(pt_ref, q_ref, kn_ref, vn_ref, lfn_ref, *rest, pp, heads, steps, page, scale):
    del pt_ref
    k_refs, v_refs, lf_refs = rest[:pp], rest[pp:2 * pp], rest[2 * pp:3 * pp]
    o_ref, m_scr, l_scr, acc_scr, carry_scr = rest[3 * pp:]
    d = LANES
    p_id = pl.program_id(1)

    @pl.when(p_id == 0)
    def _():
        m_scr[...] = jnp.full_like(m_scr, -jnp.inf)
        l_scr[...] = jnp.zeros_like(l_scr)
        acc_scr[...] = jnp.zeros_like(acc_scr)
        carry_scr[...] = jnp.zeros_like(carry_scr)

    qb = (q_ref[...] * scale).astype(BF16)

    def page_update(key, val, lf_t, mask):
        c_t = _cumsum_lanes(lf_t) + carry_scr[...]
        carry_scr[...] = jnp.broadcast_to(c_t[:, page - 1:], carry_scr.shape)
        for h in range(heads):
            rows = slice(h * steps, (h + 1) * steps)
            s = lax.dot_general(qb[:, h * d:(h + 1) * d], key(h).astype(BF16), _NT,
                                preferred_element_type=F32)
            s = s - c_t[h:h + 1, :]
            if mask is not None:
                s = jnp.where(mask, s, -jnp.inf)
            m_old = m_scr[rows, :]
            m_new = jnp.maximum(m_old, jnp.max(s, axis=-1, keepdims=True))
            alpha = jnp.exp(m_old - m_new)
            p = jnp.exp(s - m_new)
            l_scr[rows, :] = alpha * l_scr[rows, :] + jnp.sum(p, axis=-1, keepdims=True)
            acc_scr[rows, :] = alpha * acc_scr[rows, :] + jnp.dot(
                p.astype(BF16), val(h).astype(BF16), preferred_element_type=F32)
            m_scr[rows, :] = m_new

    for i in range(pp):
        page_update(lambda h, r=k_refs[i]: r[:, h, :], lambda h, r=v_refs[i]: r[:, h, :],
                    lf_refs[i][...], None)

    @pl.when(p_id == pl.num_programs(1) - 1)
    def _():
        pad = lambda a: jnp.concatenate([a, jnp.zeros((page - steps, a.shape[1]), a.dtype)], axis=0)
        qpos = lax.broadcasted_iota(jnp.int32, (steps, page), 0)
        kpos = lax.broadcasted_iota(jnp.int32, (steps, page), 1)
        page_update(lambda h: pad(kn_ref[:, h * d:(h + 1) * d]), lambda h: pad(vn_ref[:, h * d:(h + 1) * d]),
                    pad(lfn_ref[...]).T[0:heads, :], kpos <= qpos)
        for h in range(heads):
            rows = slice(h * steps, (h + 1) * steps)
            o_ref[:, h * d:(h + 1) * d] = acc_scr[rows, :] / l_scr[rows, :]


def _fox_decode(z, logf, cache_k, cache_v, cache_logf_t, page_table, *, row0, layer, heads, steps, pp):
    nseq, n_pages = page_table.shape
    page = cache_k.shape[2]
    assert page == LANES
    d = LANES
    w = heads * d
    rb0 = row0 // steps
    qcol, kcol, vcol = 4, 5, 6
    kern = functools.partial(_fox_decode_kernel, pp=pp, heads=heads, steps=steps, page=page,
                             scale=d ** -0.5)
    pg = lambda i: (lambda b, p, pt: (layer, pt[b, p * pp + i], 0, 0))
    pg5 = lambda i: (lambda b, p, pt: (layer, pt[b, p * pp + i], 0, 0, 0))
    kv_spec = [pl.BlockSpec((None, None, page, heads, d), pg5(i)) for i in range(pp)]
    lf_spec = [pl.BlockSpec((None, None, heads, page), pg(i)) for i in range(pp)]
    grid_spec = pltpu.PrefetchScalarGridSpec(
        num_scalar_prefetch=1,
        grid=(nseq, n_pages // pp),
        in_specs=[pl.BlockSpec((steps, w), lambda b, p, pt: (rb0 + b, qcol)),
                  pl.BlockSpec((steps, w), lambda b, p, pt: (rb0 + b, kcol)),
                  pl.BlockSpec((steps, w), lambda b, p, pt: (rb0 + b, vcol)),
                  pl.BlockSpec((steps, LANES), lambda b, p, pt: (rb0 + b, 0))]
                 + kv_spec + kv_spec + lf_spec,
        out_specs=pl.BlockSpec((steps, w), lambda b, p, pt: (b, 0)),
        scratch_shapes=[pltpu.VMEM((heads * steps, LANES), F32),
                        pltpu.VMEM((heads * steps, LANES), F32),
                        pltpu.VMEM((heads * steps, LANES), F32),
                        pltpu.VMEM((heads, LANES), F32)],
    )
    return pl.pallas_call(
        kern,
        out_shape=jax.ShapeDtypeStruct((nseq * steps, w), F32),
        grid_spec=grid_spec,
        compiler_params=_params("arbitrary", "arbitrary"),
        name="fox_decode",
    )(page_table, z, z, z, logf, *([cache_k] * pp), *([cache_v] * pp), *([cache_logf_t] * pp))


def _outproj_kernel(ohg_ref, ofx_ref, x_ref, wa_ref, wb_ref, npost_ref, npre_ref, x1_ref, h2_ref):
    y = (jnp.dot(ohg_ref[...], wa_ref[...], preferred_element_type=F32)
         + jnp.dot(ofx_ref[...], wb_ref[...], preferred_element_type=F32))
    x1 = x_ref[...] + _rms(y) * npost_ref[...]
    x1_ref[...] = x1
    h2_ref[...] = (_rms(x1) * npre_ref[...]).astype(BF16)


def _outproj(o_hg, o_fx, x, w_a, w_b, n_post, n_pre, *, tm):
    m, d = x.shape
    wd = o_hg.shape[1]
    row = lambda i: (i, 0)
    fixed = lambda i: (0, 0)
    return pl.pallas_call(
        _outproj_kernel,
        out_shape=(jax.ShapeDtypeStruct((m, d), F32), jax.ShapeDtypeStruct((m, d), BF16)),
        grid=(m // tm,),
        in_specs=[pl.BlockSpec((tm, wd), row), pl.BlockSpec((tm, wd), row), pl.BlockSpec((tm, d), row),
                  pl.BlockSpec((wd, d), fixed), pl.BlockSpec((wd, d), fixed),
                  pl.BlockSpec((1, d), fixed), pl.BlockSpec((1, d), fixed)],
        out_specs=(pl.BlockSpec((tm, d), row), pl.BlockSpec((tm, d), row)),
        compiler_params=_params("arbitrary"),
        name="outproj",
    )(o_hg, o_fx, x, w_a, w_b, n_post, n_pre)


def _ffn_kernel(h_ref, x_ref, wg_ref, wu_ref, wd_ref, npost_ref, o_ref):
    j = pl.program_id(1)
    h = h_ref[...]
    a = jnp.dot(h, wg_ref[...], preferred_element_type=F32)
    b = jnp.dot(h, wu_ref[...], preferred_element_type=F32)
    sig_a, _, _ = _sigmoid_parts(a)
    part = jnp.dot((a * sig_a * b).astype(BF16), wd_ref[...], preferred_element_type=F32)

    @pl.when(j == 0)
    def _():
        o_ref[...] = part

    @pl.when(j > 0)
    def _():
        o_ref[...] += part

    @pl.when(j == pl.num_programs(1) - 1)
    def _():
        o_ref[...] = x_ref[...] + _rms(o_ref[...]) * npost_ref[...]


def _ffn(h2, x1, w_g, w_u, w_d, n_post, *, tm, tf):
    m, d = x1.shape
    f = w_g.shape[1]
    return pl.pallas_call(
        _ffn_kernel,
        out_shape=jax.ShapeDtypeStruct((m, d), F32),
        grid=(m // tm, f // tf),
        in_specs=[pl.BlockSpec((tm, d), lambda i, j: (i, 0)),
                  pl.BlockSpec((tm, d), lambda i, j: (i, 0)),
                  pl.BlockSpec((d, tf), lambda i, j: (0, j)),
                  pl.BlockSpec((d, tf), lambda i, j: (0, j)),
                  pl.BlockSpec((tf, d), lambda i, j: (j, 0)),
                  pl.BlockSpec((1, d), lambda i, j: (0, 0))],
        out_specs=pl.BlockSpec((tm, d), lambda i, j: (i, 0)),
        compiler_params=_params("arbitrary", "arbitrary"),
        name="ffn",
    )(h2, x1, w_g, w_u, w_d, n_post)


def _tiles(batch, seq, nseq, steps, d_ff):
    m = batch * seq + nseq * steps
    tm = next(t for t in (1024, 512, 256, 128) if seq % t == 0 and m % t == 0)
    tm_ffn = min(tm, 512)
    tf = next(t for t in (512, 256, 128) if d_ff % t == 0)
    tt = min(seq, 256)
    tq = min(seq, 512)
    return dict(tm=tm, tm_out=min(tm, 256), tm_ffn=tm_ffn, tf=tf, tt=tt, tq=tq)


def kernel(x_prompt, x_sample, cache_k, cache_v, cache_logf, state_hgrn, page_table, w_in, b_fox_f,
           hg_lower_bounds, hg_norm_w, w_out, norm_mix_pre, norm_mix_post, norm_ffn_pre, norm_ffn_post,
           w_gate, w_up, w_down):
    batch, seq, d = x_prompt.shape
    nseq, steps, _ = x_sample.shape
    depth, n_pool, page, fh, fd = cache_k.shape
    hh, hk, hv = state_hgrn.shape[2:]
    d_ff = w_gate.shape[2]
    assert hk == LANES and hv == LANES and fd == LANES and hh == fh
    heads = hh
    n_main = 7 * heads * LANES
    assert w_in.shape[2] == n_main + fh and fh <= LANES
    mp = batch * seq
    t = _tiles(batch, seq, nseq, steps, d_ff)
    pages_per_step = 4
    assert page_table.shape[1] % pages_per_step == 0 and seq % 16 == 0

    x = jnp.concatenate([x_prompt.reshape(mp, d), x_sample.reshape(nseq * steps, d)], axis=0)
    row = lambda a: a.reshape(1, -1).astype(F32)
    cache_logf_t = cache_logf.transpose(0, 1, 3, 2)

    outs = {n: [] for n in ("kp", "vp", "gp", "sp", "ks", "vs", "gs", "ss")}
    for l in range(depth):
        w_main = w_in[l, :, :n_main].astype(BF16)
        w_ff = jnp.pad(w_in[l, :, n_main:], ((0, 0), (0, LANES - fh))).astype(BF16)
        b_ff = jnp.pad(b_fox_f[l].astype(F32), (0, LANES - fh)).reshape(1, LANES)
        z, logf, c = _inproj(x, row(norm_mix_pre[l]), w_main, w_ff, b_ff, tm=t["tm"], tn=1024, seq_len=seq)

        gn = row(hg_norm_w[l])
        lb_raw = hg_lower_bounds.astype(F32)
        ohg_p, s_p = _hgrn_prompt(z, lb_raw, gn, batch=batch, seq=seq, heads=heads, layer=l,
                                  chunk=16, tt=t["tt"])
        ohg_s, s_s = _hgrn_sample(z, lb_raw, gn, state_hgrn, row0=mp, nseq=nseq, steps=steps,
                                  heads=heads, layer=l, bb=8)

        c_t = c[:mp, :fh].reshape(batch, seq, fh).transpose(0, 2, 1).reshape(batch, fh, 1, seq)
        ofx_p = _fox_prompt(z, c_t, batch=batch, seq=seq, heads=heads, tq=t["tq"])
        ofx_s = _fox_decode(z, logf, cache_k, cache_v, cache_logf_t, page_table, row0=mp, layer=l, heads=heads,
                            steps=steps, pp=pages_per_step)

        o_hg = jnp.concatenate([ohg_p, ohg_s], axis=0)
        o_fx = jnp.concatenate([ofx_p, ofx_s.astype(BF16)], axis=0)
        w_o = w_out[l].astype(BF16)
        hw = heads * LANES
        x1, h2 = _outproj(o_hg, o_fx, x, w_o[:hw], w_o[hw:], row(norm_mix_post[l]), row(norm_ffn_pre[l]),
                          tm=t["tm_out"])
        x = _ffn(h2, x1, w_gate[l].astype(BF16), w_up[l].astype(BF16), w_down[l].astype(BF16),
                 row(norm_ffn_post[l]), tm=t["tm_ffn"], tf=t["tf"])

        kcol, vcol = 5 * heads * LANES, 6 * heads * LANES
        outs["kp"].append(z[:mp, kcol:kcol + hw].reshape(batch, seq, fh, fd))
        outs["vp"].append(z[:mp, vcol:vcol + hw].reshape(batch, seq, fh, fd))
        outs["gp"].append(logf[:mp, :fh].reshape(batch, seq, fh))
        outs["sp"].append(s_p)
        outs["ks"].append(z[mp:, kcol:kcol + hw].reshape(nseq, steps, fh, fd))
        outs["vs"].append(z[mp:, vcol:vcol + hw].reshape(nseq, steps, fh, fd))
        outs["gs"].append(logf[mp:, :fh].reshape(nseq, steps, fh))
        outs["ss"].append(s_s)

    st = {n: jnp.stack(v) for n, v in outs.items()}
    return (x[:mp].reshape(batch, seq, d), x[mp:].reshape(nseq, steps, d),
            st["kp"], st["vp"], st["gp"], st["sp"], st["ks"], st["vs"], st["gs"], st["ss"])
```

```python
import functools

import jax
import jax.numpy as jnp
from jax import lax
from jax.experimental import pallas as pl
from jax.experimental.pallas import tpu as pltpu

NORM_EPS = 1e-6
LOG2E = 1.4426950408889634
LANES = 128
SUBLANES = 8
VMEM_LIMIT_BYTES = 56 * 1024 * 1024

F32 = jnp.float32
BF16 = jnp.bfloat16

_NT = (((1,), (1,)), ((), ()))
_TN = (((0,), (0,)), ((), ()))


def _params(*sem):
    return pltpu.CompilerParams(dimension_semantics=sem, vmem_limit_bytes=VMEM_LIMIT_BYTES)


def _rms(x):
    return x * lax.rsqrt(jnp.mean(x * x, axis=-1, keepdims=True) + NORM_EPS)


def _sigmoid_parts(z):
    e = jnp.exp(-jnp.abs(z))
    r = 1.0 / (1.0 + e)
    pos = z >= 0
    return jnp.where(pos, r, e * r), jnp.where(pos, e * r, r), jnp.minimum(z, 0.0) - jnp.log1p(e)


def _cumsum_rows(x, period=None):
    n = x.shape[0]
    span = n if period is None else period
    row = lax.broadcasted_iota(jnp.int32, x.shape, 0)
    if period is not None:
        row = row % period
    s = 1
    while s < span:
        x = x + jnp.where(row >= s, pltpu.roll(x, s, axis=0), 0.0)
        s *= 2
    return x


def _cumsum_lanes(x):
    n = x.shape[-1]
    lane = lax.broadcasted_iota(jnp.int32, x.shape, x.ndim - 1)
    s = 1
    while s < n:
        x = x + jnp.where(lane >= s, pltpu.roll(x, s, axis=x.ndim - 1), 0.0)
        s *= 2
    return x


def _inproj_kernel(x_ref, nw_ref, w_ref, wff_ref, bf_ref, z_ref, logf_ref, c_ref,
                   h_scr, carry_scr, *, seq_tiles):
    i = pl.program_id(0)
    j = pl.program_id(1)

    @pl.when(j == 0)
    def _():
        hb = (_rms(x_ref[...]) * nw_ref[...]).astype(BF16)
        h_scr[...] = hb
        ff = jnp.dot(hb, wff_ref[...], preferred_element_type=F32) + bf_ref[...]
        _, _, lf = _sigmoid_parts(ff)
        logf_ref[...] = lf

        @pl.when(i % seq_tiles == 0)
        def _():
            carry_scr[...] = jnp.zeros_like(carry_scr)

        c = _cumsum_rows(lf) + carry_scr[...]
        c_ref[...] = c
        carry_scr[...] = c[c.shape[0] - 1:, :]

    z_ref[...] = jnp.dot(h_scr[...], w_ref[...], preferred_element_type=F32)


def _inproj(x, nw, w_in, w_ff, b_ff, *, layer, n, tm, tn, seq_len):
    m, d = x.shape
    kern = functools.partial(_inproj_kernel, seq_tiles=seq_len // tm)
    return pl.pallas_call(
        kern,
        out_shape=(jax.ShapeDtypeStruct((m, n), F32),
                   jax.ShapeDtypeStruct((m, LANES), F32),
                   jax.ShapeDtypeStruct((m, LANES), F32)),
        grid=(m // tm, n // tn),
        in_specs=[pl.BlockSpec((tm, d), lambda i, j: (i, 0)),
                  pl.BlockSpec((1, d), lambda i, j: (0, 0)),
                  pl.BlockSpec((None, d, tn), lambda i, j: (layer, 0, j)),
                  pl.BlockSpec((d, LANES), lambda i, j: (0, 0)),
                  pl.BlockSpec((1, LANES), lambda i, j: (0, 0))],
        out_specs=(pl.BlockSpec((tm, tn), lambda i, j: (i, j)),
                   pl.BlockSpec((tm, LANES), lambda i, j: (i, 0)),
                   pl.BlockSpec((tm, LANES), lambda i, j: (i, 0))),
        scratch_shapes=[pltpu.VMEM((tm, d), BF16), pltpu.VMEM((1, LANES), F32)],
        compiler_params=_params("arbitrary", "arbitrary"),
        name="inproj",
    )(x, nw, w_in, w_ff, b_ff)


def _lower_bound(raw, layer):
    depth = raw.shape[0]
    rows = [raw[i:i + 1, :] for i in range(depth)]
    mx = functools.reduce(jnp.maximum, rows)
    ex = [jnp.exp(r - mx) for r in rows]
    tot = functools.reduce(lambda a, b: a + b, ex)
    p = [e / tot for e in ex]
    cum = functools.reduce(lambda a, b: a + b, p[:layer + 1])
    return cum - p[0]


def _hgrn_gates(lb, qr, z, chunk):
    sig_q, _, _ = _sigmoid_parts(qr)
    q = qr * sig_q
    _, sig_nz, logsig = _sigmoid_parts(z)
    a = jnp.log(lb)
    bb = jnp.log1p(-lb) + logsig
    logf = jnp.maximum(a, bb) + jnp.log1p(jnp.exp(-jnp.abs(a - bb)))
    k = (1.0 - lb) * sig_nz
    return q, k, _cumsum_rows(logf, period=chunk)


def _hgrn_tile(q, k, v, b, chunk):
    rows = q.shape[0]
    chunks = [slice(c * chunk, (c + 1) * chunk) for c in range(rows // chunk)]
    spread = lambda r: jnp.broadcast_to(r, (chunk, r.shape[1]))
    ref = jnp.concatenate([spread(b[s.start + chunk // 2 - 1:s.start + chunk // 2]) for s in chunks], axis=0)
    ends = [b[s.stop - 1:s.stop] for s in chunks]
    end = jnp.concatenate([spread(e) for e in ends], axis=0)
    vb = v.astype(BF16)
    qi = (q * jnp.exp(b - ref)).astype(BF16)
    ki = (k * jnp.exp(ref - b)).astype(BF16)
    att = lax.dot_general(qi, ki, _NT, preferred_element_type=F32)
    r = lax.broadcasted_iota(jnp.int32, (rows, rows), 0)
    c = lax.broadcasted_iota(jnp.int32, (rows, rows), 1)
    same_chunk = jnp.bitwise_and(r, -chunk) == jnp.bitwise_and(c, -chunk)
    att = jnp.where(same_chunk & (r >= c), att, 0.0).astype(BF16)
    intra = jnp.dot(att, vb, preferred_element_type=F32)
    qs = q * jnp.exp(b)
    ks = k * jnp.exp(end - b)
    incs = [lax.dot_general(v[s].astype(BF16), ks[s].astype(BF16), _TN, preferred_element_type=F32)
            for s in chunks]
    return [(s, intra[s], qs[s].astype(BF16), jnp.exp(e), u) for s, e, u in zip(chunks, ends, incs)]


def _hgrn_finish(o, g, gn):
    sig_g, _, _ = _sigmoid_parts(g)
    return (_rms(o) * gn * (g * sig_g)).astype(BF16)


def _hgrn_prompt_kernel(lb_ref, q_ref, f_ref, i_ref, g_ref, gn_ref, o_ref, sfin_ref,
                        st_scr, o_scr, *, chunk, layer):
    t = pl.program_id(2)

    @pl.when(t == 0)
    def _():
        st_scr[...] = jnp.zeros_like(st_scr)

    lb = _lower_bound(lb_ref[...], layer)
    q, k, b = _hgrn_gates(lb, q_ref[...], f_ref[...], chunk)
    st = st_scr[...]
    for sl, intra, qs, decay, inc in _hgrn_tile(q, k, i_ref[...], b, chunk):
        o_scr[sl, :] = intra + lax.dot_general(qs, st.astype(BF16), _NT, preferred_element_type=F32)
        st = st * decay + inc
    st_scr[...] = st
    o_ref[...] = _hgrn_finish(o_scr[...], g_ref[...], gn_ref[...])

    @pl.when(t == pl.num_programs(2) - 1)
    def _():
        sfin_ref[...] = st.T


def _hgrn_prompt(z, lb_raw, gn, *, batch, seq, heads, layer, chunk, tt):
    nt = seq // tt
    col = lambda base: (lambda b, h, t: (b * nt + t, base + h))
    kern = functools.partial(_hgrn_prompt_kernel, chunk=chunk, layer=layer)
    depth = lb_raw.shape[0]
    return pl.pallas_call(
        kern,
        out_shape=(jax.ShapeDtypeStruct((batch * seq, heads * LANES), BF16),
                   jax.ShapeDtypeStruct((batch, heads, LANES, LANES), F32)),
        grid=(batch, heads, nt),
        in_specs=[pl.BlockSpec((depth, LANES), lambda b, h, t: (0, h)),
                  pl.BlockSpec((tt, LANES), col(0)),
                  pl.BlockSpec((tt, LANES), col(heads)),
                  pl.BlockSpec((tt, LANES), col(2 * heads)),
                  pl.BlockSpec((tt, LANES), col(3 * heads)),
                  pl.BlockSpec((1, LANES), lambda b, h, t: (0, 0))],
        out_specs=(pl.BlockSpec((tt, LANES), lambda b, h, t: (b * nt + t, h)),
                   pl.BlockSpec((None, None, LANES, LANES), lambda b, h, t: (b, h, 0, 0))),
        scratch_shapes=[pltpu.VMEM((LANES, LANES), F32), pltpu.VMEM((tt, LANES), F32)],
        compiler_params=_params("arbitrary", "arbitrary", "arbitrary"),
        name="hgrn_prompt",
    )(lb_raw, z, z, z, z, gn)


def _hgrn_sample_kernel(lb_ref, q_ref, f_ref, i_ref, g_ref, gn_ref, s0_ref, o_ref, sfin_ref,
                        o_scr, *, chunk, layer):
    lb = _lower_bound(lb_ref[...], layer)
    q, k, b = _hgrn_gates(lb, q_ref[...], f_ref[...], chunk)
    for s, (sl, intra, qs, decay, inc) in enumerate(_hgrn_tile(q, k, i_ref[...], b, chunk)):
        st = s0_ref[s].T
        o_scr[sl, :] = intra + lax.dot_general(qs, st.astype(BF16), _NT, preferred_element_type=F32)
        sfin_ref[s] = (st * decay + inc).T
    o_ref[...] = _hgrn_finish(o_scr[...], g_ref[...], gn_ref[...])


def _hgrn_sample(z, lb_raw, gn, state, *, row0, nseq, steps, heads, layer, bb):
    rows = bb * steps
    rb0 = row0 // rows
    col = lambda base: (lambda i, h: (rb0 + i, base + h))
    kern = functools.partial(_hgrn_sample_kernel, chunk=steps, layer=layer)
    depth = lb_raw.shape[0]
    return pl.pallas_call(
        kern,
        out_shape=(jax.ShapeDtypeStruct((nseq * steps, heads * LANES), BF16),
                   jax.ShapeDtypeStruct((nseq, heads, LANES, LANES), F32)),
        grid=(nseq // bb, heads),
        in_specs=[pl.BlockSpec((depth, LANES), lambda i, h: (0, h)),
                  pl.BlockSpec((rows, LANES), col(0)),
                  pl.BlockSpec((rows, LANES), col(heads)),
                  pl.BlockSpec((rows, LANES), col(2 * heads)),
                  pl.BlockSpec((rows, LANES), col(3 * heads)),
                  pl.BlockSpec((1, LANES), lambda i, h: (0, 0)),
                  pl.BlockSpec((None, bb, None, LANES, LANES), lambda i, h: (layer, i, h, 0, 0))],
        out_specs=(pl.BlockSpec((rows, LANES), lambda i, h: (i, h)),
                   pl.BlockSpec((bb, None, LANES, LANES), lambda i, h: (i, h, 0, 0))),
        scratch_shapes=[pltpu.VMEM((rows, LANES), F32)],
        compiler_params=_params("arbitrary", "arbitrary"),
        name="hgrn_sample",
    )(lb_raw, z, z, z, z, gn, state)


def _softmax_step(s, v, m, l, acc):
    m_new = jnp.maximum(m, jnp.max(s, axis=-1, keepdims=True))
    alpha = jnp.exp2(m - m_new)
    p = jnp.exp2(s - m_new)
    l = alpha * l + jnp.sum(p, axis=-1, keepdims=True)
    acc = alpha * acc + jnp.dot(p.astype(BF16), v, preferred_element_type=F32)
    return m_new, l, acc


def _fox_prompt_kernel(q_ref, k_ref, v_ref, ck_ref, o_ref, *, tq, scale):
    qi = pl.program_id(2)
    q = (q_ref[...] * (scale * LOG2E)).astype(BF16)

    def scores(j):
        start = pl.multiple_of(j * tq, tq)
        kj = k_ref[pl.ds(start, tq), :].astype(BF16)
        vj = v_ref[pl.ds(start, tq), :].astype(BF16)
        s = lax.dot_general(q, kj, _NT, preferred_element_type=F32)
        return s - ck_ref[:, pl.ds(start, tq)] * LOG2E, vj

    def body(j, carry):
        s, vj = scores(j)
        return _softmax_step(s, vj, *carry)

    init = (jnp.full((tq, 1), -jnp.inf, F32), jnp.zeros((tq, 1), F32), jnp.zeros((tq, LANES), F32))
    carry = lax.fori_loop(0, qi, body, init)
    s, vj = scores(qi)
    causal = (lax.broadcasted_iota(jnp.int32, (tq, tq), 0) >= lax.broadcasted_iota(jnp.int32, (tq, tq), 1))
    _, l, acc = _softmax_step(jnp.where(causal, s, -jnp.inf), vj, *carry)
    o_ref[...] = (acc / l).astype(BF16)


def _fox_prompt(z, c_t, *, batch, seq, heads, tq):
    nq = seq // tq
    d = LANES
    qcol, kcol, vcol = 4 * heads, 5 * heads, 6 * heads
    kern = functools.partial(_fox_prompt_kernel, tq=tq, scale=d ** -0.5)
    return pl.pallas_call(
        kern,
        out_shape=jax.ShapeDtypeStruct((batch * seq, heads * d), BF16),
        grid=(batch, heads, nq),
        in_specs=[pl.BlockSpec((tq, d), lambda b, h, i: (b * nq + i, qcol + h)),
                  pl.BlockSpec((seq, d), lambda b, h, i: (b, kcol + h)),
                  pl.BlockSpec((seq, d), lambda b, h, i: (b, vcol + h)),
                  pl.BlockSpec((None, None, 1, seq), lambda b, h, i: (b, h, 0, 0))],
        out_specs=pl.BlockSpec((tq, d), lambda b, h, i: (b * nq + i, h)),
        compiler_params=_params("arbitrary", "arbitrary", "arbitrary"),
        name="fox_prompt",
    )(z, z, z, c_t)


def _fox_decode_kernel(pt_ref, q_ref, kn_ref, vn_ref, lfn_ref, *rest, n_pages, heads, steps, page, scale):
    del pt_ref
    k_refs, v_refs, lf_refs = rest[:n_pages], rest[n_pages:2 * n_pages], rest[2 * n_pages:3 * n_pages]
    o_ref = rest[3 * n_pages]
    d = LANES
    q = q_ref[...] * scale

    lf_pages = [r[...] for r in lf_refs]
    pad = lambda a: jnp.concatenate([a, jnp.zeros((page - steps, a.shape[1]), a.dtype)], axis=0)
    lf_pages.append(pad(lfn_ref[...]).T[0:heads, :])
    c_pages, before = [], jnp.zeros((heads, 1), F32)
    for lf in lf_pages:
        c_pages.append(_cumsum_lanes(lf) + before)
        before = before + jnp.sum(lf, axis=-1, keepdims=True)

    qpos = lax.broadcasted_iota(jnp.int32, (steps, page), 0)
    kpos = lax.broadcasted_iota(jnp.int32, (steps, page), 1)
    causal = kpos <= qpos

    for h in range(heads):
        cols = slice(h * d, (h + 1) * d)
        qh = q[:, cols].astype(BF16)
        head_rows = pl.ds(h, page, stride=heads)
        keys = [r[head_rows, :] for r in k_refs] + [pad(kn_ref[:, cols])]
        vals = [r[head_rows, :] for r in v_refs] + [pad(vn_ref[:, cols])]
        s = [lax.dot_general(qh, k.astype(BF16), _NT, preferred_element_type=F32) - c[h:h + 1, :]
             for k, c in zip(keys, c_pages)]
        s[-1] = jnp.where(causal, s[-1], -jnp.inf)
        m = functools.reduce(jnp.maximum, s)
        m = jnp.max(m, axis=-1, keepdims=True)
        p = [jnp.exp(t - m) for t in s]
        l = jnp.sum(functools.reduce(lambda a, b: a + b, p), axis=-1, keepdims=True)
        acc = functools.reduce(lambda a, b: a + b,
                               [jnp.dot(t.astype(BF16), v.astype(BF16), preferred_element_type=F32)
                                for t, v in zip(p, vals)])
        o_ref[:, cols] = acc / l


def _fox_decode(z, logf, cache_k, cache_v, cache_logf_t, page_table, *, row0, layer, heads, steps):
    nseq, n_pages = page_table.shape
    depth, n_pool, page = cache_k.shape[:3]
    assert page == LANES
    d = LANES
    w = heads * d
    rb0 = row0 // steps
    qcol, kcol, vcol = 4, 5, 6
    ck = cache_k.reshape(depth, n_pool, page * heads, d)
    cv = cache_v.reshape(depth, n_pool, page * heads, d)
    kern = functools.partial(_fox_decode_kernel, n_pages=n_pages, heads=heads, steps=steps, page=page,
                             scale=d ** -0.5)
    pg = lambda i: (lambda b, pt: (layer, pt[b, i], 0, 0))
    kv_spec = [pl.BlockSpec((None, None, page * heads, d), pg(i)) for i in range(n_pages)]
    lf_spec = [pl.BlockSpec((None, None, heads, page), pg(i)) for i in range(n_pages)]
    grid_spec = pltpu.PrefetchScalarGridSpec(
        num_scalar_prefetch=1,
        grid=(nseq,),
        in_specs=[pl.BlockSpec((steps, w), lambda b, pt: (rb0 + b, qcol)),
                  pl.BlockSpec((steps, w), lambda b, pt: (rb0 + b, kcol)),
                  pl.BlockSpec((steps, w), lambda b, pt: (rb0 + b, vcol)),
                  pl.BlockSpec((steps, LANES), lambda b, pt: (rb0 + b, 0))]
                 + kv_spec + kv_spec + lf_spec,
        out_specs=pl.BlockSpec((steps, w), lambda b, pt: (b, 0)),
    )
    return pl.pallas_call(
        kern,
        out_shape=jax.ShapeDtypeStruct((nseq * steps, w), F32),
        grid_spec=grid_spec,
        compiler_params=_params("arbitrary"),
        name="fox_decode",
    )(page_table, z, z, z, logf, *([ck] * n_pages), *([cv] * n_pages), *([cache_logf_t] * n_pages))


def _outproj_kernel(ohg_ref, ofx_ref, x_ref, wa_ref, wb_ref, npost_ref, npre_ref, x1_ref, h2_ref):
    y = (jnp.dot(ohg_ref[...], wa_ref[...], preferred_element_type=F32)
         + jnp.dot(ofx_ref[...], wb_ref[...], preferred_element_type=F32))
    x1 = x_ref[...] + _rms(y) * npost_ref[...]
    x1_ref[...] = x1
    h2_ref[...] = (_rms(x1) * npre_ref[...]).astype(BF16)


def _outproj(o_hg, o_fx, x, w_out, n_post, n_pre, *, layer, tm):
    m, d = x.shape
    wd = o_hg.shape[1]
    row = lambda i: (i, 0)
    fixed = lambda i: (0, 0)
    return pl.pallas_call(
        _outproj_kernel,
        out_shape=(jax.ShapeDtypeStruct((m, d), F32), jax.ShapeDtypeStruct((m, d), BF16)),
        grid=(m // tm,),
        in_specs=[pl.BlockSpec((tm, wd), row), pl.BlockSpec((tm, wd), row), pl.BlockSpec((tm, d), row),
                  pl.BlockSpec((None, wd, d), lambda i: (layer, 0, 0)),
                  pl.BlockSpec((None, wd, d), lambda i: (layer, 1, 0)),
                  pl.BlockSpec((1, d), fixed), pl.BlockSpec((1, d), fixed)],
        out_specs=(pl.BlockSpec((tm, d), row), pl.BlockSpec((tm, d), row)),
        compiler_params=_params("arbitrary"),
        name="outproj",
    )(o_hg, o_fx, x, w_out, w_out, n_post, n_pre)


def _ffn_kernel(h_ref, x_ref, wg_ref, wu_ref, wd_ref, npost_ref, o_ref):
    j = pl.program_id(1)

    @pl.when(j == 0)
    def _():
        o_ref[...] = jnp.zeros_like(o_ref)

    h = h_ref[...]
    a = jnp.dot(h, wg_ref[...], preferred_element_type=F32)
    b = jnp.dot(h, wu_ref[...], preferred_element_type=F32)
    sig_a, _, _ = _sigmoid_parts(a)
    o_ref[...] += jnp.dot((a * sig_a * b).astype(BF16), wd_ref[...], preferred_element_type=F32)

    @pl.when(j == pl.num_programs(1) - 1)
    def _():
        o_ref[...] = x_ref[...] + _rms(o_ref[...]) * npost_ref[...]


def _ffn(h2, x1, w_g, w_u, w_d, n_post, *, layer, tm, tf):
    m, d = x1.shape
    f = w_g.shape[2]
    return pl.pallas_call(
        _ffn_kernel,
        out_shape=jax.ShapeDtypeStruct((m, d), F32),
        grid=(m // tm, f // tf),
        in_specs=[pl.BlockSpec((tm, d), lambda i, j: (i, 0)),
                  pl.BlockSpec((tm, d), lambda i, j: (i, 0)),
                  pl.BlockSpec((None, d, tf), lambda i, j: (layer, 0, j)),
                  pl.BlockSpec((None, d, tf), lambda i, j: (layer, 0, j)),
                  pl.BlockSpec((None, tf, d), lambda i, j: (layer, j, 0)),
                  pl.BlockSpec((1, d), lambda i, j: (0, 0))],
        out_specs=pl.BlockSpec((tm, d), lambda i, j: (i, 0)),
        compiler_params=_params("arbitrary", "arbitrary"),
        name="ffn",
    )(h2, x1, w_g, w_u, w_d, n_post)


def _tiles(batch, seq, nseq, steps, d_ff):
    m = batch * seq + nseq * steps
    tm = next(t for t in (1024, 512, 256, 128) if seq % t == 0 and m % t == 0)
    tm_ffn = min(tm, 512)
    tf = next(t for t in (512, 256, 128) if d_ff % t == 0)
    tt = min(seq, 256)
    tq = min(seq, 1024)
    return dict(tm=tm, tm_out=min(tm, 256), tm_ffn=tm_ffn, tf=tf, tt=tt, tq=tq)


def kernel(x_prompt, x_sample, cache_k, cache_v, cache_logf, state_hgrn, page_table, w_in, b_fox_f,
           hg_lower_bounds, hg_norm_w, w_out, norm_mix_pre, norm_mix_post, norm_ffn_pre, norm_ffn_post,
           w_gate, w_up, w_down):
    batch, seq, d = x_prompt.shape
    nseq, steps, _ = x_sample.shape
    depth, n_pool, page, fh, fd = cache_k.shape
    hh, hk, hv = state_hgrn.shape[2:]
    d_ff = w_gate.shape[2]
    assert hk == LANES and hv == LANES and fd == LANES and hh == fh
    heads = hh
    n_main = 7 * heads * LANES
    assert w_in.shape[2] == n_main + fh and fh <= LANES
    mp = batch * seq
    t = _tiles(batch, seq, nseq, steps, d_ff)
    assert seq % 16 == 0

    x = jnp.concatenate([x_prompt.reshape(mp, d), x_sample.reshape(nseq * steps, d)], axis=0)
    row = lambda a: a.reshape(1, -1).astype(F32)
    cache_logf_t = cache_logf.transpose(0, 1, 3, 2)
    w_in_b, w_out_b = w_in.astype(BF16), w_out.astype(BF16)
    w_gate_b, w_up_b, w_down_b = w_gate.astype(BF16), w_up.astype(BF16), w_down.astype(BF16)

    outs = {n: [] for n in ("kp", "vp", "gp", "sp", "ks", "vs", "gs", "ss")}
    for l in range(depth):
        w_ff = jnp.pad(w_in[l, :, n_main:], ((0, 0), (0, LANES - fh))).astype(BF16)
        b_ff = jnp.pad(b_fox_f[l].astype(F32), (0, LANES - fh)).reshape(1, LANES)
        z, logf, c = _inproj(x, row(norm_mix_pre[l]), w_in_b, w_ff, b_ff, layer=l, n=n_main, tm=t["tm"],
                             tn=1024, seq_len=seq)

        gn = row(hg_norm_w[l])
        lb_raw = hg_lower_bounds.astype(F32)
        ohg_p, s_p = _hgrn_prompt(z, lb_raw, gn, batch=batch, seq=seq, heads=heads, layer=l,
                                  chunk=16, tt=t["tt"])
        ohg_s, s_s = _hgrn_sample(z, lb_raw, gn, state_hgrn, row0=mp, nseq=nseq, steps=steps,
                                  heads=heads, layer=l, bb=8)

        c_t = c[:mp, :fh].reshape(batch, seq, fh).transpose(0, 2, 1).reshape(batch, fh, 1, seq)
        ofx_p = _fox_prompt(z, c_t, batch=batch, seq=seq, heads=heads, tq=t["tq"])
        ofx_s = _fox_decode(z, logf, cache_k, cache_v, cache_logf_t, page_table, row0=mp, layer=l, heads=heads,
                            steps=steps)

        o_hg = jnp.concatenate([ohg_p, ohg_s], axis=0)
        o_fx = jnp.concatenate([ofx_p, ofx_s.astype(BF16)], axis=0)
        hw = heads * LANES
        x1, h2 = _outproj(o_hg, o_fx, x, w_out_b, row(norm_mix_post[l]), row(norm_ffn_pre[l]),
                          layer=l, tm=t["tm_out"])
        x = _ffn(h2, x1, w_gate_b, w_up_b, w_down_b, row(norm_ffn_post[l]), layer=l,
                 tm=t["tm_ffn"], tf=t["tf"])

        kcol, vcol = 5 * heads * LANES, 6 * heads * LANES
        outs["kp"].append(z[:mp, kcol:kcol + hw].reshape(batch, seq, fh, fd))
        outs["vp"].append(z[:mp, vcol:vcol + hw].reshape(batch, seq, fh, fd))
        outs["gp"].append(logf[:mp, :fh].reshape(batch, seq, fh))
        outs["sp"].append(s_p)
        outs["ks"].append(z[mp:, kcol:kcol + hw].reshape(nseq, steps, fh, fd))
        outs["vs"].append(z[mp:, vcol:vcol + hw].reshape(nseq, steps, fh, fd))
        outs["gs"].append(logf[mp:, :fh].reshape(nseq, steps, fh))
        outs["ss"].append(s_s)

    st = {n: jnp.stack(v) for n, v in outs.items()}
    return (x[:mp].reshape(batch, seq, d), x[mp:].reshape(nseq, steps, d),
            st["kp"], st["vp"], st["gp"], st["sp"], st["ks"], st["vs"], st["gs"], st["ss"])
```

```python
import functools

import jax
import jax.numpy as jnp
from jax import lax
from jax.experimental import pallas as pl
from jax.experimental.pallas import tpu as pltpu

NORM_EPS = 1e-6
LOG2E = 1.4426950408889634
LANES = 128
SUBLANES = 8
VMEM_LIMIT_BYTES = 56 * 1024 * 1024
HGRN_CHUNK = 16
HGRN_ATT_ROWS = 256

F32 = jnp.float32
BF16 = jnp.bfloat16

_NT = (((1,), (1,)), ((), ()))
_TN = (((0,), (0,)), ((), ()))


def _params(*sem):
    return pltpu.CompilerParams(dimension_semantics=sem, vmem_limit_bytes=VMEM_LIMIT_BYTES)


def _rms(x):
    return x * lax.rsqrt(jnp.mean(x * x, axis=-1, keepdims=True) + NORM_EPS)


def _sigmoid_parts(z):
    e = jnp.exp(-jnp.abs(z))
    r = 1.0 / (1.0 + e)
    pos = z >= 0
    return jnp.where(pos, r, e * r), jnp.where(pos, e * r, r), jnp.minimum(z, 0.0) - jnp.log1p(e)


def _cumsum_rows(x, period=None):
    n = x.shape[0]
    span = n if period is None else period
    row = lax.broadcasted_iota(jnp.int32, x.shape, 0)
    if period is not None:
        row = row % period
    s = 1
    while s < span:
        x = x + jnp.where(row >= s, pltpu.roll(x, s, axis=0), 0.0)
        s *= 2
    return x


def _cumsum_lanes(x):
    n = x.shape[-1]
    lane = lax.broadcasted_iota(jnp.int32, x.shape, x.ndim - 1)
    s = 1
    while s < n:
        x = x + jnp.where(lane >= s, pltpu.roll(x, s, axis=x.ndim - 1), 0.0)
        s *= 2
    return x


def _inproj_kernel(x_ref, nw_ref, w_ref, wff_ref, bf_ref, z_ref, logf_ref, c_ref,
                   h_scr, carry_scr, *, seq_tiles):
    i = pl.program_id(0)
    j = pl.program_id(1)

    @pl.when(j == 0)
    def _():
        hb = (_rms(x_ref[...]) * nw_ref[...]).astype(BF16)
        h_scr[...] = hb
        ff = jnp.dot(hb, wff_ref[...], preferred_element_type=F32) + bf_ref[...]
        _, _, lf = _sigmoid_parts(ff)
        logf_ref[...] = lf

        @pl.when(i % seq_tiles == 0)
        def _():
            carry_scr[...] = jnp.zeros_like(carry_scr)

        c = _cumsum_rows(lf) + carry_scr[...]
        c_ref[...] = c
        carry_scr[...] = c[c.shape[0] - 1:, :]

    z_ref[...] = jnp.dot(h_scr[...], w_ref[...], preferred_element_type=F32)


def _inproj(x, nw, w_in, w_ff, b_ff, *, layer, n, tm, tn, seq_len):
    m, d = x.shape
    kern = functools.partial(_inproj_kernel, seq_tiles=seq_len // tm)
    return pl.pallas_call(
        kern,
        out_shape=(jax.ShapeDtypeStruct((m, n), F32),
                   jax.ShapeDtypeStruct((m, LANES), F32),
                   jax.ShapeDtypeStruct((m, LANES), F32)),
        grid=(m // tm, n // tn),
        in_specs=[pl.BlockSpec((tm, d), lambda i, j: (i, 0)),
                  pl.BlockSpec((1, d), lambda i, j: (0, 0)),
                  pl.BlockSpec((None, d, tn), lambda i, j: (layer, 0, j)),
                  pl.BlockSpec((d, LANES), lambda i, j: (0, 0)),
                  pl.BlockSpec((1, LANES), lambda i, j: (0, 0))],
        out_specs=(pl.BlockSpec((tm, tn), lambda i, j: (i, j)),
                   pl.BlockSpec((tm, LANES), lambda i, j: (i, 0)),
                   pl.BlockSpec((tm, LANES), lambda i, j: (i, 0))),
        scratch_shapes=[pltpu.VMEM((tm, d), BF16), pltpu.VMEM((1, LANES), F32)],
        compiler_params=_params("arbitrary", "arbitrary"),
        name="inproj",
    )(x, nw, w_in, w_ff, b_ff)


def _lower_bound(raw, layer):
    depth = raw.shape[0]
    rows = [raw[i:i + 1, :] for i in range(depth)]
    mx = functools.reduce(jnp.maximum, rows)
    ex = [jnp.exp(r - mx) for r in rows]
    tot = functools.reduce(lambda a, b: a + b, ex)
    p = [e / tot for e in ex]
    cum = functools.reduce(lambda a, b: a + b, p[:layer + 1])
    return cum - p[0]


def _hgrn_gates(lb, qr, z, chunk):
    sig_q, _, _ = _sigmoid_parts(qr)
    q = qr * sig_q
    _, sig_nz, logsig = _sigmoid_parts(z)
    a = jnp.log(lb)
    bb = jnp.log1p(-lb) + logsig
    logf = jnp.maximum(a, bb) + jnp.log1p(jnp.exp(-jnp.abs(a - bb)))
    k = (1.0 - lb) * sig_nz
    return q, k, _cumsum_rows(logf, period=chunk)


def _hgrn_tile(q, k, v, b, chunk):
    rows = q.shape[0]
    chunks = [slice(c * chunk, (c + 1) * chunk) for c in range(rows // chunk)]
    spread = lambda r: jnp.broadcast_to(r, (chunk, r.shape[1]))
    ref = jnp.concatenate([spread(b[s.start + chunk // 2 - 1:s.start + chunk // 2]) for s in chunks], axis=0)
    ends = [b[s.stop - 1:s.stop] for s in chunks]
    end = jnp.concatenate([spread(e) for e in ends], axis=0)
    vb = v.astype(BF16)
    qi = (q * jnp.exp(b - ref)).astype(BF16)
    ki = (k * jnp.exp(ref - b)).astype(BF16)
    g = min(rows, HGRN_ATT_ROWS)
    r = lax.broadcasted_iota(jnp.int32, (g, g), 0)
    c = lax.broadcasted_iota(jnp.int32, (g, g), 1)
    keep = (jnp.bitwise_and(r, -chunk) == jnp.bitwise_and(c, -chunk)) & (r >= c)
    parts = []
    for s in (slice(i, i + g) for i in range(0, rows, g)):
        att = lax.dot_general(qi[s], ki[s], _NT, preferred_element_type=F32)
        parts.append(jnp.dot(jnp.where(keep, att, 0.0).astype(BF16), vb[s], preferred_element_type=F32))
    intra = jnp.concatenate(parts, axis=0)
    qs = q * jnp.exp(b)
    ks = k * jnp.exp(end - b)
    incs = [lax.dot_general(v[s].astype(BF16), ks[s].astype(BF16), _TN, preferred_element_type=F32)
            for s in chunks]
    return [(s, intra[s], qs[s].astype(BF16), jnp.exp(e), u) for s, e, u in zip(chunks, ends, incs)]


def _hgrn_finish(o, g, gn):
    sig_g, _, _ = _sigmoid_parts(g)
    return (_rms(o) * gn * (g * sig_g)).astype(BF16)


def _hgrn_prompt_kernel(lb_ref, q_ref, f_ref, i_ref, g_ref, gn_ref, o_ref, sfin_ref,
                        st_scr, o_scr, *, chunk, layer):
    t = pl.program_id(2)

    @pl.when(t == 0)
    def _():
        st_scr[...] = jnp.zeros_like(st_scr)

    lb = _lower_bound(lb_ref[...], layer)
    q, k, b = _hgrn_gates(lb, q_ref[...], f_ref[...], chunk)
    st = st_scr[...]
    for sl, intra, qs, decay, inc in _hgrn_tile(q, k, i_ref[...], b, chunk):
        o_scr[sl, :] = intra + lax.dot_general(qs, st.astype(BF16), _NT, preferred_element_type=F32)
        st = st * decay + inc
    st_scr[...] = st
    o_ref[...] = _hgrn_finish(o_scr[...], g_ref[...], gn_ref[...])

    @pl.when(t == pl.num_programs(2) - 1)
    def _():
        sfin_ref[...] = st.T


def _hgrn_prompt(z, lb_raw, gn, *, batch, seq, heads, layer, chunk, tt):
    nt = seq // tt
    col = lambda base: (lambda b, h, t: (b * nt + t, base + h))
    kern = functools.partial(_hgrn_prompt_kernel, chunk=chunk, layer=layer)
    depth = lb_raw.shape[0]
    return pl.pallas_call(
        kern,
        out_shape=(jax.ShapeDtypeStruct((batch * seq, heads * LANES), BF16),
                   jax.ShapeDtypeStruct((batch, heads, LANES, LANES), F32)),
        grid=(batch, heads, nt),
        in_specs=[pl.BlockSpec((depth, LANES), lambda b, h, t: (0, h)),
                  pl.BlockSpec((tt, LANES), col(0)),
                  pl.BlockSpec((tt, LANES), col(heads)),
                  pl.BlockSpec((tt, LANES), col(2 * heads)),
                  pl.BlockSpec((tt, LANES), col(3 * heads)),
                  pl.BlockSpec((1, LANES), lambda b, h, t: (0, 0))],
        out_specs=(pl.BlockSpec((tt, LANES), lambda b, h, t: (b * nt + t, h)),
                   pl.BlockSpec((None, None, LANES, LANES), lambda b, h, t: (b, h, 0, 0))),
        scratch_shapes=[pltpu.VMEM((LANES, LANES), F32), pltpu.VMEM((tt, LANES), F32)],
        compiler_params=_params("arbitrary", "arbitrary", "arbitrary"),
        name="hgrn_prompt",
    )(lb_raw, z, z, z, z, gn)


def _hgrn_sample_kernel(lb_ref, q_ref, f_ref, i_ref, g_ref, gn_ref, s0_ref, *rest, chunk, layer):
    o_ref, sfin_ref, o_scr = rest[-3:]
    lb = _lower_bound(lb_ref[...], layer)
    q, k, b = _hgrn_gates(lb, q_ref[...], f_ref[...], chunk)
    for s, (sl, intra, qs, decay, inc) in enumerate(_hgrn_tile(q, k, i_ref[...], b, chunk)):
        st = s0_ref[s].T
        o_scr[sl, :] = intra + lax.dot_general(qs, st.astype(BF16), _NT, preferred_element_type=F32)
        sfin_ref[s] = (st * decay + inc).T
    o_ref[...] = _hgrn_finish(o_scr[...], g_ref[...], gn_ref[...])


def _hgrn_sample(z, lb_raw, gn, state, stacked, *, row0, nseq, steps, heads, layer, bb):
    rows = bb * steps
    rb0 = row0 // rows
    col = lambda base: (lambda i, h: (rb0 + i, base + h))
    kern = functools.partial(_hgrn_sample_kernel, chunk=steps, layer=layer)
    depth = lb_raw.shape[0]
    state_spec = pl.BlockSpec((None, bb, None, LANES, LANES), lambda i, h: (layer, i, h, 0, 0))
    in_specs = [pl.BlockSpec((depth, LANES), lambda i, h: (0, h)),
                pl.BlockSpec((rows, LANES), col(0)),
                pl.BlockSpec((rows, LANES), col(heads)),
                pl.BlockSpec((rows, LANES), col(2 * heads)),
                pl.BlockSpec((rows, LANES), col(3 * heads)),
                pl.BlockSpec((1, LANES), lambda i, h: (0, 0)),
                state_spec]
    args = [lb_raw, z, z, z, z, gn, state]
    aliases = {}
    if stacked is not None:
        in_specs.append(pl.BlockSpec(memory_space=pl.ANY))
        aliases = {len(args): 1}
        args.append(stacked)
    return pl.pallas_call(
        kern,
        out_shape=(jax.ShapeDtypeStruct((nseq * steps, heads * LANES), BF16),
                   jax.ShapeDtypeStruct(state.shape, F32)),
        grid=(nseq // bb, heads),
        in_specs=in_specs,
        out_specs=(pl.BlockSpec((rows, LANES), lambda i, h: (i, h)), state_spec),
        scratch_shapes=[pltpu.VMEM((rows, LANES), F32)],
        input_output_aliases=aliases,
        compiler_params=_params("arbitrary", "arbitrary"),
        name="hgrn_sample",
    )(*args)


def _softmax_step(s, v, m, l, acc):
    m_new = jnp.maximum(m, jnp.max(s, axis=-1, keepdims=True))
    alpha = jnp.exp2(m - m_new)
    p = jnp.exp2(s - m_new)
    l = alpha * l + jnp.sum(p, axis=-1, keepdims=True)
    acc = alpha * acc + jnp.dot(p.astype(BF16), v, preferred_element_type=F32)
    return m_new, l, acc


def _fox_prompt_kernel(q_ref, k_ref, v_ref, ck_ref, o_ref, *, tq, scale):
    qi = pl.program_id(2)
    q = (q_ref[...] * (scale * LOG2E)).astype(BF16)

    def scores(j):
        start = pl.multiple_of(j * tq, tq)
        kj = k_ref[pl.ds(start, tq), :].astype(BF16)
        vj = v_ref[pl.ds(start, tq), :].astype(BF16)
        s = lax.dot_general(q, kj, _NT, preferred_element_type=F32)
        return s - ck_ref[:, pl.ds(start, tq)] * LOG2E, vj

    def body(j, carry):
        s, vj = scores(j)
        return _softmax_step(s, vj, *carry)

    init = (jnp.full((tq, 1), -jnp.inf, F32), jnp.zeros((tq, 1), F32), jnp.zeros((tq, LANES), F32))
    carry = lax.fori_loop(0, qi, body, init)
    s, vj = scores(qi)
    causal = (lax.broadcasted_iota(jnp.int32, (tq, tq), 0) >= lax.broadcasted_iota(jnp.int32, (tq, tq), 1))
    _, l, acc = _softmax_step(jnp.where(causal, s, -jnp.inf), vj, *carry)
    o_ref[...] = (acc / l).astype(BF16)


def _fox_prompt(z, c_t, *, batch, seq, heads, tq):
    nq = seq // tq
    d = LANES
    qcol, kcol, vcol = 4 * heads, 5 * heads, 6 * heads
    kern = functools.partial(_fox_prompt_kernel, tq=tq, scale=d ** -0.5)
    return pl.pallas_call(
        kern,
        out_shape=jax.ShapeDtypeStruct((batch * seq, heads * d), BF16),
        grid=(batch, heads, nq),
        in_specs=[pl.BlockSpec((tq, d), lambda b, h, i: (b * nq + i, qcol + h)),
                  pl.BlockSpec((seq, d), lambda b, h, i: (b, kcol + h)),
                  pl.BlockSpec((seq, d), lambda b, h, i: (b, vcol + h)),
                  pl.BlockSpec((None, None, 1, seq), lambda b, h, i: (b, h, 0, 0))],
        out_specs=pl.BlockSpec((tq, d), lambda b, h, i: (b * nq + i, h)),
        compiler_params=_params("arbitrary", "arbitrary", "arbitrary"),
        name="fox_prompt",
    )(z, z, z, c_t)


def _fox_decode_kernel(pt_ref, q_ref, kn_ref, vn_ref, lfn_ref, *rest, n_pages, heads, steps, page, scale):
    del pt_ref
    k_refs, v_refs, lf_refs = rest[:n_pages], rest[n_pages:2 * n_pages], rest[2 * n_pages:3 * n_pages]
    o_ref = rest[3 * n_pages]
    d = LANES
    q = q_ref[...] * scale

    lf_pages = [r[...] for r in lf_refs]
    pad = lambda a: jnp.concatenate([a, jnp.zeros((page - steps, a.shape[1]), a.dtype)], axis=0)
    lf_pages.append(pad(lfn_ref[...]).T[0:heads, :])
    c_pages, before = [], jnp.zeros((heads, 1), F32)
    for lf in lf_pages:
        c_pages.append(_cumsum_lanes(lf) + before)
        before = before + jnp.sum(lf, axis=-1, keepdims=True)

    qpos = lax.broadcasted_iota(jnp.int32, (steps, page), 0)
    kpos = lax.broadcasted_iota(jnp.int32, (steps, page), 1)
    causal = kpos <= qpos

    for h in range(heads):
        cols = slice(h * d, (h + 1) * d)
        qh = q[:, cols].astype(BF16)
        head_rows = pl.ds(h, page, stride=heads)
        keys = [r[head_rows, :] for r in k_refs] + [pad(kn_ref[:, cols])]
        vals = [r[head_rows, :] for r in v_refs] + [pad(vn_ref[:, cols])]
        s = [lax.dot_general(qh, k.astype(BF16), _NT, preferred_element_type=F32) - c[h:h + 1, :]
             for k, c in zip(keys, c_pages)]
        s[-1] = jnp.where(causal, s[-1], -jnp.inf)
        m = functools.reduce(jnp.maximum, s)
        m = jnp.max(m, axis=-1, keepdims=True)
        p = [jnp.exp(t - m) for t in s]
        l = jnp.sum(functools.reduce(lambda a, b: a + b, p), axis=-1, keepdims=True)
        acc = functools.reduce(lambda a, b: a + b,
                               [jnp.dot(t.astype(BF16), v.astype(BF16), preferred_element_type=F32)
                                for t, v in zip(p, vals)])
        o_ref[:, cols] = acc / l


def _fox_decode(z, logf, cache_k, cache_v, cache_logf_t, page_table, *, row0, layer, heads, steps):
    nseq, n_pages = page_table.shape
    depth, n_pool, page = cache_k.shape[:3]
    assert page == LANES
    d = LANES
    w = heads * d
    rb0 = row0 // steps
    qcol, kcol, vcol = 4, 5, 6
    ck = cache_k.reshape(depth, n_pool, page * heads, d)
    cv = cache_v.reshape(depth, n_pool, page * heads, d)
    kern = functools.partial(_fox_decode_kernel, n_pages=n_pages, heads=heads, steps=steps, page=page,
                             scale=d ** -0.5)
    pg = lambda i: (lambda b, pt: (layer, pt[b, i], 0, 0))
    kv_spec = [pl.BlockSpec((None, None, page * heads, d), pg(i)) for i in range(n_pages)]
    lf_spec = [pl.BlockSpec((None, None, heads, page), pg(i)) for i in range(n_pages)]
    grid_spec = pltpu.PrefetchScalarGridSpec(
        num_scalar_prefetch=1,
        grid=(nseq,),
        in_specs=[pl.BlockSpec((steps, w), lambda b, pt: (rb0 + b, qcol)),
                  pl.BlockSpec((steps, w), lambda b, pt: (rb0 + b, kcol)),
                  pl.BlockSpec((steps, w), lambda b, pt: (rb0 + b, vcol)),
                  pl.BlockSpec((steps, LANES), lambda b, pt: (rb0 + b, 0))]
                 + kv_spec + kv_spec + lf_spec,
        out_specs=pl.BlockSpec((steps, w), lambda b, pt: (b, 0)),
    )
    return pl.pallas_call(
        kern,
        out_shape=jax.ShapeDtypeStruct((nseq * steps, w), F32),
        grid_spec=grid_spec,
        compiler_params=_params("arbitrary"),
        name="fox_decode",
    )(page_table, z, z, z, logf, *([ck] * n_pages), *([cv] * n_pages), *([cache_logf_t] * n_pages))


def _kv_rows_kernel(k_ref, v_ref, *rest, heads):
    ko_ref, vo_ref = rest[-2:]
    rows = k_ref.shape[0]
    for src, dst in ((k_ref, ko_ref), (v_ref, vo_ref)):
        for h in range(heads):
            dst[pl.ds(h, rows, stride=heads), :] = src[:, h * LANES:(h + 1) * LANES]


def _kv_rows(z, stacked, *, depth, rows, heads, layer, tr):
    w = heads * LANES
    kcol, vcol = 5, 6
    shape = jax.ShapeDtypeStruct((depth, rows * heads, LANES), F32)
    out_spec = pl.BlockSpec((None, tr * heads, LANES), lambda i: (layer, i, 0))
    in_specs = [pl.BlockSpec((tr, w), lambda i: (i, kcol)), pl.BlockSpec((tr, w), lambda i: (i, vcol))]
    args = [z, z]
    aliases = {}
    if stacked is not None:
        in_specs += [pl.BlockSpec(memory_space=pl.ANY)] * 2
        aliases = {2: 0, 3: 1}
        args += list(stacked)
    return pl.pallas_call(
        functools.partial(_kv_rows_kernel, heads=heads),
        out_shape=(shape, shape),
        grid=(rows // tr,),
        in_specs=in_specs,
        out_specs=(out_spec, out_spec),
        input_output_aliases=aliases,
        compiler_params=_params("arbitrary"),
        name="kv_rows",
    )(*args)


def _outproj_kernel(ohp_ref, ohs_ref, ofp_ref, ofs_ref, x_ref, wa_ref, wb_ref, npost_ref, npre_ref,
                    x1_ref, h2_ref, *, prompt_tiles):
    is_prompt = pl.program_id(0) < prompt_tiles
    o_hg = jnp.where(is_prompt, ohp_ref[...], ohs_ref[...])
    o_fx = jnp.where(is_prompt, ofp_ref[...], ofs_ref[...].astype(BF16))
    y = (jnp.dot(o_hg, wa_ref[...], preferred_element_type=F32)
         + jnp.dot(o_fx, wb_ref[...], preferred_element_type=F32))
    x1 = x_ref[...] + _rms(y) * npost_ref[...]
    x1_ref[...] = x1
    h2_ref[...] = (_rms(x1) * npre_ref[...]).astype(BF16)


def _outproj(ohg_p, ohg_s, ofx_p, ofx_s, x, w_out, n_post, n_pre, *, layer, tm):
    m, d = x.shape
    wd = ohg_p.shape[1]
    prompt_tiles = ohg_p.shape[0] // tm
    row = lambda i: (i, 0)
    prow = lambda i: (jnp.minimum(i, prompt_tiles - 1), 0)
    srow = lambda i: (jnp.maximum(i - prompt_tiles, 0), 0)
    fixed = lambda i: (0, 0)
    return pl.pallas_call(
        functools.partial(_outproj_kernel, prompt_tiles=prompt_tiles),
        out_shape=(jax.ShapeDtypeStruct((m, d), F32), jax.ShapeDtypeStruct((m, d), BF16)),
        grid=(m // tm,),
        in_specs=[pl.BlockSpec((tm, wd), prow), pl.BlockSpec((tm, wd), srow),
                  pl.BlockSpec((tm, wd), prow), pl.BlockSpec((tm, wd), srow),
                  pl.BlockSpec((tm, d), row),
                  pl.BlockSpec((None, wd, d), lambda i: (layer, 0, 0)),
                  pl.BlockSpec((None, wd, d), lambda i: (layer, 1, 0)),
                  pl.BlockSpec((1, d), fixed), pl.BlockSpec((1, d), fixed)],
        out_specs=(pl.BlockSpec((tm, d), row), pl.BlockSpec((tm, d), row)),
        compiler_params=_params("arbitrary"),
        name="outproj",
    )(ohg_p, ohg_s, ofx_p, ofx_s, x, w_out, w_out, n_post, n_pre)


def _ffn_kernel(h_ref, x_ref, wg_ref, wu_ref, wd_ref, npost_ref, o_ref):
    j = pl.program_id(1)

    @pl.when(j == 0)
    def _():
        o_ref[...] = jnp.zeros_like(o_ref)

    h = h_ref[...]
    a = jnp.dot(h, wg_ref[...], preferred_element_type=F32)
    b = jnp.dot(h, wu_ref[...], preferred_element_type=F32)
    sig_a, _, _ = _sigmoid_parts(a)
    o_ref[...] += jnp.dot((a * sig_a * b).astype(BF16), wd_ref[...], preferred_element_type=F32)

    @pl.when(j == pl.num_programs(1) - 1)
    def _():
        o_ref[...] = x_ref[...] + _rms(o_ref[...]) * npost_ref[...]


def _ffn(h2, x1, w_g, w_u, w_d, n_post, *, layer, tm, tf):
    m, d = x1.shape
    f = w_g.shape[2]
    return pl.pallas_call(
        _ffn_kernel,
        out_shape=jax.ShapeDtypeStruct((m, d), F32),
        grid=(m // tm, f // tf),
        in_specs=[pl.BlockSpec((tm, d), lambda i, j: (i, 0)),
                  pl.BlockSpec((tm, d), lambda i, j: (i, 0)),
                  pl.BlockSpec((None, d, tf), lambda i, j: (layer, 0, j)),
                  pl.BlockSpec((None, d, tf), lambda i, j: (layer, 0, j)),
                  pl.BlockSpec((None, tf, d), lambda i, j: (layer, j, 0)),
                  pl.BlockSpec((1, d), lambda i, j: (0, 0))],
        out_specs=pl.BlockSpec((tm, d), lambda i, j: (i, 0)),
        compiler_params=_params("arbitrary", "arbitrary"),
        name="ffn",
    )(h2, x1, w_g, w_u, w_d, n_post)


def _tiles(batch, seq, nseq, steps, d_ff):
    m = batch * seq + nseq * steps
    tm = next(t for t in (1024, 512, 256, 128) if seq % t == 0 and m % t == 0)
    tm_out = next(t for t in (256, 128) if (batch * seq) % t == 0 and (nseq * steps) % t == 0)
    tm_ffn = next(t for t in (768, 512, 256, 128) if m % t == 0)
    tf = next(t for t in (512, 256, 128) if d_ff % t == 0)
    tt = next(t for t in (512, 256, 128, HGRN_CHUNK) if seq % t == 0)
    tq = next(t for t in (1024, 512, 256, 128) if seq % t == 0)
    bb = next(b for b in (16, 8, 4, 2, 1) if nseq % b == 0 and (batch * seq) % (b * steps) == 0)
    tr = next(t for t in (512, 256, 128) if (batch * seq) % t == 0)
    return dict(tm=tm, tn=1024, tm_out=tm_out, tm_ffn=tm_ffn, tf=tf, tt=tt, tq=tq, bb=bb, tr=tr)


def kernel(x_prompt, x_sample, cache_k, cache_v, cache_logf, state_hgrn, page_table, w_in, b_fox_f,
           hg_lower_bounds, hg_norm_w, w_out, norm_mix_pre, norm_mix_post, norm_ffn_pre, norm_ffn_post,
           w_gate, w_up, w_down):
    batch, seq, d = x_prompt.shape
    nseq, steps, _ = x_sample.shape
    depth, n_pool, page, fh, fd = cache_k.shape
    hh, hk, hv = state_hgrn.shape[2:]
    d_ff = w_gate.shape[2]
    assert hk == LANES and hv == LANES and fd == LANES and hh == fh
    heads = hh
    n_main = 7 * heads * LANES
    assert w_in.shape[2] == n_main + fh and fh <= LANES
    mp = batch * seq
    t = _tiles(batch, seq, nseq, steps, d_ff)
    assert seq % 16 == 0

    x = jnp.concatenate([x_prompt.reshape(mp, d), x_sample.reshape(nseq * steps, d)], axis=0)
    row = lambda a: a.reshape(1, -1).astype(F32)
    cache_logf_t = cache_logf.transpose(0, 1, 3, 2)
    w_in_b, w_out_b = w_in.astype(BF16), w_out.astype(BF16)
    w_gate_b, w_up_b, w_down_b = w_gate.astype(BF16), w_up.astype(BF16), w_down.astype(BF16)

    outs = {n: [] for n in ("gp", "sp", "ks", "vs", "gs")}
    hgrn_sample_states = kv_prompt = None
    for l in range(depth):
        w_ff = jnp.pad(w_in[l, :, n_main:], ((0, 0), (0, LANES - fh))).astype(BF16)
        b_ff = jnp.pad(b_fox_f[l].astype(F32), (0, LANES - fh)).reshape(1, LANES)
        z, logf, c = _inproj(x, row(norm_mix_pre[l]), w_in_b, w_ff, b_ff, layer=l, n=n_main, tm=t["tm"],
                             tn=t["tn"], seq_len=seq)

        gn = row(hg_norm_w[l])
        lb_raw = hg_lower_bounds.astype(F32)
        ohg_p, s_p = _hgrn_prompt(z, lb_raw, gn, batch=batch, seq=seq, heads=heads, layer=l,
                                  chunk=HGRN_CHUNK, tt=t["tt"])
        ohg_s, hgrn_sample_states = _hgrn_sample(z, lb_raw, gn, state_hgrn, hgrn_sample_states, row0=mp,
                                                 nseq=nseq, steps=steps, heads=heads, layer=l, bb=t["bb"])

        c_t = c[:mp, :fh].reshape(batch, seq, fh).transpose(0, 2, 1).reshape(batch, fh, 1, seq)
        ofx_p = _fox_prompt(z, c_t, batch=batch, seq=seq, heads=heads, tq=t["tq"])
        ofx_s = _fox_decode(z, logf, cache_k, cache_v, cache_logf_t, page_table, row0=mp, layer=l, heads=heads,
                            steps=steps)

        hw = heads * LANES
        x1, h2 = _outproj(ohg_p, ohg_s, ofx_p, ofx_s, x, w_out_b, row(norm_mix_post[l]), row(norm_ffn_pre[l]),
                          layer=l, tm=t["tm_out"])
        x = _ffn(h2, x1, w_gate_b, w_up_b, w_down_b, row(norm_ffn_post[l]), layer=l,
                 tm=t["tm_ffn"], tf=t["tf"])

        kcol, vcol = 5 * heads * LANES, 6 * heads * LANES
        kv_prompt = _kv_rows(z, kv_prompt, depth=depth, rows=mp, heads=heads, layer=l, tr=t["tr"])
        outs["gp"].append(logf[:mp, :fh].reshape(batch, seq, fh))
        outs["sp"].append(s_p)
        outs["ks"].append(z[mp:, kcol:kcol + hw].reshape(nseq, steps, fh, fd))
        outs["vs"].append(z[mp:, vcol:vcol + hw].reshape(nseq, steps, fh, fd))
        outs["gs"].append(logf[mp:, :fh].reshape(nseq, steps, fh))

    st = {n: jnp.stack(v) for n, v in outs.items()}
    k_prompt, v_prompt = (a.reshape(depth, batch, seq, fh, fd) for a in kv_prompt)
    return (x[:mp].reshape(batch, seq, d), x[mp:].reshape(nseq, steps, d),
            k_prompt, v_prompt, st["gp"], st["sp"], st["ks"], st["vs"], st["gs"], hgrn_sample_states)
```

```python
import functools

import jax
import jax.numpy as jnp
from jax import lax
from jax.experimental import pallas as pl
from jax.experimental.pallas import tpu as pltpu

NORM_EPS = 1e-6
LOG2E = 1.4426950408889634
LANES = 128
SUBLANES = 8
VMEM_LIMIT_BYTES = 56 * 1024 * 1024
HGRN_CHUNK = 16
HGRN_ATT_ROWS = 256

F32 = jnp.float32
BF16 = jnp.bfloat16

_NT = (((1,), (1,)), ((), ()))
_TN = (((0,), (0,)), ((), ()))


def _params(*sem):
    return pltpu.CompilerParams(dimension_semantics=sem, vmem_limit_bytes=VMEM_LIMIT_BYTES)


def _rms(x):
    return x * lax.rsqrt(jnp.mean(x * x, axis=-1, keepdims=True) + NORM_EPS)


def _sigmoid_parts(z):
    e = jnp.exp(-jnp.abs(z))
    r = 1.0 / (1.0 + e)
    pos = z >= 0
    return jnp.where(pos, r, e * r), jnp.where(pos, e * r, r), jnp.minimum(z, 0.0) - jnp.log1p(e)


def _cumsum_rows(x, period=None):
    n = x.shape[0]
    span = n if period is None else period
    row = lax.broadcasted_iota(jnp.int32, x.shape, 0)
    if period is not None:
        row = row % period
    s = 1
    while s < span:
        x = x + jnp.where(row >= s, pltpu.roll(x, s, axis=0), 0.0)
        s *= 2
    return x


def _cumsum_lanes(x):
    n = x.shape[-1]
    lane = lax.broadcasted_iota(jnp.int32, x.shape, x.ndim - 1)
    s = 1
    while s < n:
        x = x + jnp.where(lane >= s, pltpu.roll(x, s, axis=x.ndim - 1), 0.0)
        s *= 2
    return x


def _inproj_kernel(x_ref, nw_ref, w_ref, wff_ref, bf_ref, z_ref, logf_ref, c_ref,
                   h_scr, carry_scr, *, seq_tiles):
    i = pl.program_id(0)
    j = pl.program_id(1)

    @pl.when(j == 0)
    def _():
        hb = (_rms(x_ref[...]) * nw_ref[...]).astype(BF16)
        h_scr[...] = hb
        ff = jnp.dot(hb, wff_ref[...], preferred_element_type=F32) + bf_ref[...]
        _, _, lf = _sigmoid_parts(ff)
        logf_ref[...] = lf

        @pl.when(i % seq_tiles == 0)
        def _():
            carry_scr[...] = jnp.zeros_like(carry_scr)

        c = _cumsum_rows(lf) + carry_scr[...]
        c_ref[...] = c
        carry_scr[...] = c[c.shape[0] - 1:, :]

    z_ref[...] = jnp.dot(h_scr[...], w_ref[...], preferred_element_type=F32)


def _inproj(x, nw, w_in, w_ff, b_ff, *, layer, n, tm, tn, seq_len):
    m, d = x.shape
    kern = functools.partial(_inproj_kernel, seq_tiles=seq_len // tm)
    return pl.pallas_call(
        kern,
        out_shape=(jax.ShapeDtypeStruct((m, n), F32),
                   jax.ShapeDtypeStruct((m, LANES), F32),
                   jax.ShapeDtypeStruct((m, LANES), F32)),
        grid=(m // tm, n // tn),
        in_specs=[pl.BlockSpec((tm, d), lambda i, j: (i, 0)),
                  pl.BlockSpec((1, d), lambda i, j: (0, 0)),
                  pl.BlockSpec((None, d, tn), lambda i, j: (layer, 0, j)),
                  pl.BlockSpec((d, LANES), lambda i, j: (0, 0)),
                  pl.BlockSpec((1, LANES), lambda i, j: (0, 0))],
        out_specs=(pl.BlockSpec((tm, tn), lambda i, j: (i, j)),
                   pl.BlockSpec((tm, LANES), lambda i, j: (i, 0)),
                   pl.BlockSpec((tm, LANES), lambda i, j: (i, 0))),
        scratch_shapes=[pltpu.VMEM((tm, d), BF16), pltpu.VMEM((1, LANES), F32)],
        compiler_params=_params("arbitrary", "arbitrary"),
        name="inproj",
    )(x, nw, w_in, w_ff, b_ff)


def _lower_bound(raw, layer):
    depth = raw.shape[0]
    rows = [raw[i:i + 1, :] for i in range(depth)]
    mx = functools.reduce(jnp.maximum, rows)
    ex = [jnp.exp(r - mx) for r in rows]
    tot = functools.reduce(lambda a, b: a + b, ex)
    p = [e / tot for e in ex]
    cum = functools.reduce(lambda a, b: a + b, p[:layer + 1])
    return cum - p[0]


def _hgrn_gates(lb, qr, z, chunk):
    sig_q, _, _ = _sigmoid_parts(qr)
    q = qr * sig_q
    _, sig_nz, logsig = _sigmoid_parts(z)
    a = jnp.log(lb)
    bb = jnp.log1p(-lb) + logsig
    logf = jnp.maximum(a, bb) + jnp.log1p(jnp.exp(-jnp.abs(a - bb)))
    k = (1.0 - lb) * sig_nz
    return q, k, _cumsum_rows(logf, period=chunk)


def _hgrn_tile(q, k, v, b, chunk):
    rows = q.shape[0]
    chunks = [slice(c * chunk, (c + 1) * chunk) for c in range(rows // chunk)]
    spread = lambda r: jnp.broadcast_to(r, (chunk, r.shape[1]))
    ref = jnp.concatenate([spread(b[s.start + chunk // 2 - 1:s.start + chunk // 2]) for s in chunks], axis=0)
    ends = [b[s.stop - 1:s.stop] for s in chunks]
    end = jnp.concatenate([spread(e) for e in ends], axis=0)
    vb = v.astype(BF16)
    qi = (q * jnp.exp(b - ref)).astype(BF16)
    ki = (k * jnp.exp(ref - b)).astype(BF16)
    g = min(rows, HGRN_ATT_ROWS)
    r = lax.broadcasted_iota(jnp.int32, (g, g), 0)
    c = lax.broadcasted_iota(jnp.int32, (g, g), 1)
    keep = (jnp.bitwise_and(r, -chunk) == jnp.bitwise_and(c, -chunk)) & (r >= c)
    parts = []
    for s in (slice(i, i + g) for i in range(0, rows, g)):
        att = lax.dot_general(qi[s], ki[s], _NT, preferred_element_type=F32)
        parts.append(jnp.dot(jnp.where(keep, att, 0.0).astype(BF16), vb[s], preferred_element_type=F32))
    intra = jnp.concatenate(parts, axis=0)
    qs = q * jnp.exp(b)
    ks = k * jnp.exp(end - b)
    incs = [lax.dot_general(v[s].astype(BF16), ks[s].astype(BF16), _TN, preferred_element_type=F32)
            for s in chunks]
    return [(s, intra[s], qs[s].astype(BF16), jnp.exp(e), u) for s, e, u in zip(chunks, ends, incs)]


def _hgrn_finish(o, g, gn):
    sig_g, _, _ = _sigmoid_parts(g)
    return (_rms(o) * gn * (g * sig_g)).astype(BF16)


def _hgrn_prompt_kernel(lb_ref, q_ref, f_ref, i_ref, g_ref, gn_ref, o_ref, sfin_ref,
                        st_scr, o_scr, *, chunk, layer):
    t = pl.program_id(2)

    @pl.when(t == 0)
    def _():
        st_scr[...] = jnp.zeros_like(st_scr)

    finals = []
    for j in range(st_scr.shape[0]):
        cols = slice(j * LANES, (j + 1) * LANES)
        lb = _lower_bound(lb_ref[:, cols], layer)
        q, k, b = _hgrn_gates(lb, q_ref[:, cols], f_ref[:, cols], chunk)
        st = st_scr[j]
        for sl, intra, qs, decay, inc in _hgrn_tile(q, k, i_ref[:, cols], b, chunk):
            o_scr[sl, cols] = intra + lax.dot_general(qs, st.astype(BF16), _NT, preferred_element_type=F32)
            st = st * decay + inc
        st_scr[j] = st
        finals.append(st)
        o_ref[:, cols] = _hgrn_finish(o_scr[:, cols], g_ref[:, cols], gn_ref[...])

    @pl.when(t == pl.num_programs(2) - 1)
    def _():
        for j, st in enumerate(finals):
            sfin_ref[j] = st.T


def _hgrn_prompt(z, lb_raw, gn, *, batch, seq, heads, layer, chunk, tt, hp):
    nt = seq // tt
    groups = heads // hp
    w = hp * LANES
    col = lambda part: (lambda b, h, t: (b * nt + t, part * groups + h))
    kern = functools.partial(_hgrn_prompt_kernel, chunk=chunk, layer=layer)
    depth = lb_raw.shape[0]
    return pl.pallas_call(
        kern,
        out_shape=(jax.ShapeDtypeStruct((batch * seq, heads * LANES), BF16),
                   jax.ShapeDtypeStruct((batch, heads, LANES, LANES), F32)),
        grid=(batch, groups, nt),
        in_specs=[pl.BlockSpec((depth, w), lambda b, h, t: (0, h)),
                  pl.BlockSpec((tt, w), col(0)),
                  pl.BlockSpec((tt, w), col(1)),
                  pl.BlockSpec((tt, w), col(2)),
                  pl.BlockSpec((tt, w), col(3)),
                  pl.BlockSpec((1, LANES), lambda b, h, t: (0, 0))],
        out_specs=(pl.BlockSpec((tt, w), lambda b, h, t: (b * nt + t, h)),
                   pl.BlockSpec((None, hp, LANES, LANES), lambda b, h, t: (b, h, 0, 0))),
        scratch_shapes=[pltpu.VMEM((hp, LANES, LANES), F32), pltpu.VMEM((tt, w), F32)],
        compiler_params=_params("arbitrary", "arbitrary", "arbitrary"),
        name="hgrn_prompt",
    )(lb_raw, z, z, z, z, gn)


def _hgrn_sample_kernel(lb_ref, q_ref, f_ref, i_ref, g_ref, gn_ref, s0_ref, *rest, chunk, layer):
    o_ref, sfin_ref, o_scr = rest[-3:]
    lb = _lower_bound(lb_ref[...], layer)
    q, k, b = _hgrn_gates(lb, q_ref[...], f_ref[...], chunk)
    for s, (sl, intra, qs, decay, inc) in enumerate(_hgrn_tile(q, k, i_ref[...], b, chunk)):
        st = s0_ref[s].T
        o_scr[sl, :] = intra + lax.dot_general(qs, st.astype(BF16), _NT, preferred_element_type=F32)
        sfin_ref[s] = (st * decay + inc).T
    o_ref[...] = _hgrn_finish(o_scr[...], g_ref[...], gn_ref[...])


def _hgrn_sample(z, lb_raw, gn, state, stacked, *, row0, nseq, steps, heads, layer, bb):
    rows = bb * steps
    rb0 = row0 // rows
    col = lambda base: (lambda i, h: (rb0 + i, base + h))
    kern = functools.partial(_hgrn_sample_kernel, chunk=steps, layer=layer)
    depth = lb_raw.shape[0]
    state_spec = pl.BlockSpec((None, bb, None, LANES, LANES), lambda i, h: (layer, i, h, 0, 0))
    in_specs = [pl.BlockSpec((depth, LANES), lambda i, h: (0, h)),
                pl.BlockSpec((rows, LANES), col(0)),
                pl.BlockSpec((rows, LANES), col(heads)),
                pl.BlockSpec((rows, LANES), col(2 * heads)),
                pl.BlockSpec((rows, LANES), col(3 * heads)),
                pl.BlockSpec((1, LANES), lambda i, h: (0, 0)),
                state_spec]
    args = [lb_raw, z, z, z, z, gn, state]
    aliases = {}
    if stacked is not None:
        in_specs.append(pl.BlockSpec(memory_space=pl.ANY))
        aliases = {len(args): 1}
        args.append(stacked)
    return pl.pallas_call(
        kern,
        out_shape=(jax.ShapeDtypeStruct((nseq * steps, heads * LANES), BF16),
                   jax.ShapeDtypeStruct(state.shape, F32)),
        grid=(nseq // bb, heads),
        in_specs=in_specs,
        out_specs=(pl.BlockSpec((rows, LANES), lambda i, h: (i, h)), state_spec),
        scratch_shapes=[pltpu.VMEM((rows, LANES), F32)],
        input_output_aliases=aliases,
        compiler_params=_params("arbitrary", "arbitrary"),
        name="hgrn_sample",
    )(*args)


def _softmax_step(s, v, m, l, acc):
    m_new = jnp.maximum(m, jnp.max(s, axis=-1, keepdims=True))
    alpha = jnp.exp2(m - m_new)
    p = jnp.exp2(s - m_new)
    l = alpha * l + jnp.sum(p, axis=-1, keepdims=True)
    acc = alpha * acc + jnp.dot(p.astype(BF16), v, preferred_element_type=F32)
    return m_new, l, acc


def _fox_prompt_kernel(q_ref, k_ref, v_ref, ck_ref, o_ref, *, tq, scale):
    qi = pl.program_id(2)
    q = (q_ref[...] * (scale * LOG2E)).astype(BF16)

    def rows(j):
        return pl.ds(pl.multiple_of(j * tq, tq), tq)

    def scores(j):
        kj = k_ref[rows(j), :].astype(BF16)
        return lax.dot_general(q, kj, _NT, preferred_element_type=F32) - ck_ref[:, rows(j)] * LOG2E

    def body(j, stats):
        return _softmax_step(scores(j), v_ref[rows(j), :].astype(BF16), *stats)

    init = (jnp.full((tq, 1), -jnp.inf, F32), jnp.zeros((tq, 1), F32), jnp.zeros((tq, LANES), F32))
    stats = lax.fori_loop(0, qi, body, init)
    causal = (lax.broadcasted_iota(jnp.int32, (tq, tq), 0) >= lax.broadcasted_iota(jnp.int32, (tq, tq), 1))
    _, l, acc = _softmax_step(jnp.where(causal, scores(qi), -jnp.inf), v_ref[rows(qi), :].astype(BF16), *stats)
    o_ref[...] = (acc / l).astype(BF16)


def _fox_prompt(z, c_t, *, batch, seq, heads, tq):
    nq = seq // tq
    d = LANES
    qcol, kcol, vcol = 4 * heads, 5 * heads, 6 * heads
    kern = functools.partial(_fox_prompt_kernel, tq=tq, scale=d ** -0.5)
    return pl.pallas_call(
        kern,
        out_shape=jax.ShapeDtypeStruct((batch * seq, heads * d), BF16),
        grid=(batch, heads, nq),
        in_specs=[pl.BlockSpec((tq, d), lambda b, h, i: (b * nq + i, qcol + h)),
                  pl.BlockSpec((seq, d), lambda b, h, i: (b, kcol + h)),
                  pl.BlockSpec((seq, d), lambda b, h, i: (b, vcol + h)),
                  pl.BlockSpec((None, None, 1, seq), lambda b, h, i: (b, h, 0, 0))],
        out_specs=pl.BlockSpec((tq, d), lambda b, h, i: (b * nq + i, h)),
        compiler_params=_params("arbitrary", "arbitrary", "arbitrary"),
        name="fox_prompt",
    )(z, z, z, c_t)


def _fox_decode_kernel(pt_ref, q_ref, kn_ref, vn_ref, lfn_ref, *rest, n_pages, heads, steps, page, scale):
    del pt_ref
    k_refs, v_refs, lf_refs = rest[:n_pages], rest[n_pages:2 * n_pages], rest[2 * n_pages:3 * n_pages]
    o_ref = rest[3 * n_pages]
    d = LANES
    q = q_ref[...] * scale

    lf_pages = [r[...] for r in lf_refs]
    pad = lambda a: jnp.concatenate([a, jnp.zeros((page - steps, a.shape[1]), a.dtype)], axis=0)
    lf_pages.append(pad(lfn_ref[...]).T[0:heads, :])
    c_pages, before = [], jnp.zeros((heads, 1), F32)
    for lf in lf_pages:
        c_pages.append(_cumsum_lanes(lf) + before)
        before = before + jnp.sum(lf, axis=-1, keepdims=True)

    qpos = lax.broadcasted_iota(jnp.int32, (steps, page), 0)
    kpos = lax.broadcasted_iota(jnp.int32, (steps, page), 1)
    causal = kpos <= qpos

    def scores(h):
        cols = slice(h * d, (h + 1) * d)
        qh = q[:, cols].astype(BF16)
        keys = [r[pl.ds(h, page, stride=heads), :] for r in k_refs] + [pad(kn_ref[:, cols])]
        s = [lax.dot_general(qh, k.astype(BF16), _NT, preferred_element_type=F32) - c[h:h + 1, :]
             for k, c in zip(keys, c_pages)]
        s[-1] = jnp.where(causal, s[-1], -jnp.inf)
        return s

    def attend(h, s):
        cols = slice(h * d, (h + 1) * d)
        vals = [r[pl.ds(h, page, stride=heads), :] for r in v_refs] + [pad(vn_ref[:, cols])]
        m = functools.reduce(jnp.maximum, s)
        m = jnp.max(m, axis=-1, keepdims=True)
        p = [jnp.exp(t - m) for t in s]
        l = jnp.sum(functools.reduce(lambda a, b: a + b, p), axis=-1, keepdims=True)
        acc = functools.reduce(lambda a, b: a + b,
                               [jnp.dot(t.astype(BF16), v.astype(BF16), preferred_element_type=F32)
                                for t, v in zip(p, vals)])
        o_ref[:, cols] = acc / l

    ahead = min(2, heads)
    pending = [scores(h) for h in range(ahead)]
    for h in range(heads):
        if h + ahead < heads:
            pending.append(scores(h + ahead))
        attend(h, pending.pop(0))


def _fox_decode(z, logf, cache_k, cache_v, cache_logf_t, page_table, *, row0, layer, heads, steps):
    nseq, n_pages = page_table.shape
    depth, n_pool, page = cache_k.shape[:3]
    assert page == LANES
    d = LANES
    w = heads * d
    rb0 = row0 // steps
    qcol, kcol, vcol = 4, 5, 6
    ck = cache_k.reshape(depth, n_pool, page * heads, d)
    cv = cache_v.reshape(depth, n_pool, page * heads, d)
    kern = functools.partial(_fox_decode_kernel, n_pages=n_pages, heads=heads, steps=steps, page=page,
                             scale=d ** -0.5)
    pg = lambda i: (lambda b, pt: (layer, pt[b, i], 0, 0))
    kv_spec = [pl.BlockSpec((None, None, page * heads, d), pg(i)) for i in range(n_pages)]
    lf_spec = [pl.BlockSpec((None, None, heads, page), pg(i)) for i in range(n_pages)]
    grid_spec = pltpu.PrefetchScalarGridSpec(
        num_scalar_prefetch=1,
        grid=(nseq,),
        in_specs=[pl.BlockSpec((steps, w), lambda b, pt: (rb0 + b, qcol)),
                  pl.BlockSpec((steps, w), lambda b, pt: (rb0 + b, kcol)),
                  pl.BlockSpec((steps, w), lambda b, pt: (rb0 + b, vcol)),
                  pl.BlockSpec((steps, LANES), lambda b, pt: (rb0 + b, 0))]
                 + kv_spec + kv_spec + lf_spec,
        out_specs=pl.BlockSpec((steps, w), lambda b, pt: (b, 0)),
    )
    return pl.pallas_call(
        kern,
        out_shape=jax.ShapeDtypeStruct((nseq * steps, w), F32),
        grid_spec=grid_spec,
        compiler_params=_params("arbitrary"),
        name="fox_decode",
    )(page_table, z, z, z, logf, *([ck] * n_pages), *([cv] * n_pages), *([cache_logf_t] * n_pages))


def _kv_rows_kernel(k_ref, v_ref, *rest, heads):
    ko_ref, vo_ref = rest[-2:]
    rows = k_ref.shape[0]
    for src, dst in ((k_ref, ko_ref), (v_ref, vo_ref)):
        for h in range(heads):
            dst[pl.ds(h, rows, stride=heads), :] = src[:, h * LANES:(h + 1) * LANES]


def _kv_rows(z, stacked, *, depth, rows, heads, layer, tr):
    w = heads * LANES
    kcol, vcol = 5, 6
    shape = jax.ShapeDtypeStruct((depth, rows * heads, LANES), F32)
    out_spec = pl.BlockSpec((None, tr * heads, LANES), lambda i: (layer, i, 0))
    in_specs = [pl.BlockSpec((tr, w), lambda i: (i, kcol)), pl.BlockSpec((tr, w), lambda i: (i, vcol))]
    args = [z, z]
    aliases = {}
    if stacked is not None:
        in_specs += [pl.BlockSpec(memory_space=pl.ANY)] * 2
        aliases = {2: 0, 3: 1}
        args += list(stacked)
    return pl.pallas_call(
        functools.partial(_kv_rows_kernel, heads=heads),
        out_shape=(shape, shape),
        grid=(rows // tr,),
        in_specs=in_specs,
        out_specs=(out_spec, out_spec),
        input_output_aliases=aliases,
        compiler_params=_params("arbitrary"),
        name="kv_rows",
    )(*args)


def _outproj_kernel(ohp_ref, ohs_ref, ofp_ref, ofs_ref, x_ref, wa_ref, wb_ref, npost_ref, npre_ref,
                    x1_ref, h2_ref, *, prompt_tiles):
    is_prompt = pl.program_id(0) < prompt_tiles
    o_hg = jnp.where(is_prompt, ohp_ref[...], ohs_ref[...])
    o_fx = jnp.where(is_prompt, ofp_ref[...], ofs_ref[...].astype(BF16))
    y = (jnp.dot(o_hg, wa_ref[...], preferred_element_type=F32)
         + jnp.dot(o_fx, wb_ref[...], preferred_element_type=F32))
    x1 = x_ref[...] + _rms(y) * npost_ref[...]
    x1_ref[...] = x1
    h2_ref[...] = (_rms(x1) * npre_ref[...]).astype(BF16)


def _outproj(ohg_p, ohg_s, ofx_p, ofx_s, x, w_out, n_post, n_pre, *, layer, tm):
    m, d = x.shape
    wd = ohg_p.shape[1]
    prompt_tiles = ohg_p.shape[0] // tm
    row = lambda i: (i, 0)
    prow = lambda i: (jnp.minimum(i, prompt_tiles - 1), 0)
    srow = lambda i: (jnp.maximum(i - prompt_tiles, 0), 0)
    fixed = lambda i: (0, 0)
    return pl.pallas_call(
        functools.partial(_outproj_kernel, prompt_tiles=prompt_tiles),
        out_shape=(jax.ShapeDtypeStruct((m, d), F32), jax.ShapeDtypeStruct((m, d), BF16)),
        grid=(m // tm,),
        in_specs=[pl.BlockSpec((tm, wd), prow), pl.BlockSpec((tm, wd), srow),
                  pl.BlockSpec((tm, wd), prow), pl.BlockSpec((tm, wd), srow),
                  pl.BlockSpec((tm, d), row),
                  pl.BlockSpec((None, wd, d), lambda i: (layer, 0, 0)),
                  pl.BlockSpec((None, wd, d), lambda i: (layer, 1, 0)),
                  pl.BlockSpec((1, d), fixed), pl.BlockSpec((1, d), fixed)],
        out_specs=(pl.BlockSpec((tm, d), row), pl.BlockSpec((tm, d), row)),
        compiler_params=_params("arbitrary"),
        name="outproj",
    )(ohg_p, ohg_s, ofx_p, ofx_s, x, w_out, w_out, n_post, n_pre)


def _ffn_kernel(h_ref, x_ref, wg_ref, wu_ref, wd_ref, npost_ref, o_ref):
    j = pl.program_id(1)

    @pl.when(j == 0)
    def _():
        o_ref[...] = jnp.zeros_like(o_ref)

    h = h_ref[...]
    a = jnp.dot(h, wg_ref[...], preferred_element_type=F32)
    b = jnp.dot(h, wu_ref[...], preferred_element_type=F32)
    sig_a, _, _ = _sigmoid_parts(a)
    o_ref[...] += jnp.dot((a * sig_a * b).astype(BF16), wd_ref[...], preferred_element_type=F32)

    @pl.when(j == pl.num_programs(1) - 1)
    def _():
        o_ref[...] = x_ref[...] + _rms(o_ref[...]) * npost_ref[...]


def _ffn(h2, x1, w_g, w_u, w_d, n_post, *, layer, tm, tf):
    m, d = x1.shape
    f = w_g.shape[2]
    return pl.pallas_call(
        _ffn_kernel,
        out_shape=jax.ShapeDtypeStruct((m, d), F32),
        grid=(m // tm, f // tf),
        in_specs=[pl.BlockSpec((tm, d), lambda i, j: (i, 0)),
                  pl.BlockSpec((tm, d), lambda i, j: (i, 0)),
                  pl.BlockSpec((None, d, tf), lambda i, j: (layer, 0, j)),
                  pl.BlockSpec((None, d, tf), lambda i, j: (layer, 0, j)),
                  pl.BlockSpec((None, tf, d), lambda i, j: (layer, j, 0)),
                  pl.BlockSpec((1, d), lambda i, j: (0, 0))],
        out_specs=pl.BlockSpec((tm, d), lambda i, j: (i, 0)),
        compiler_params=_params("arbitrary", "arbitrary"),
        name="ffn",
    )(h2, x1, w_g, w_u, w_d, n_post)


def _tiles(batch, seq, nseq, steps, d_ff):
    m = batch * seq + nseq * steps
    tm = next(t for t in (1024, 512, 256, 128) if seq % t == 0 and m % t == 0)
    tm_out = next(t for t in (512, 256, 128) if (batch * seq) % t == 0 and (nseq * steps) % t == 0)
    tm_ffn = next(t for t in (768, 512, 256, 128) if m % t == 0)
    tf = next(t for t in (512, 256, 128) if d_ff % t == 0)
    tt = next(t for t in (512, 256, 128, HGRN_CHUNK) if seq % t == 0)
    tq = next(t for t in (1024, 512, 256, 128) if seq % t == 0)
    bb = next(b for b in (16, 8, 4, 2, 1) if nseq % b == 0 and (batch * seq) % (b * steps) == 0)
    tr = next(t for t in (512, 256, 128) if (batch * seq) % t == 0)
    return dict(tm=tm, tn=1024, tm_out=tm_out, tm_ffn=tm_ffn, tf=tf, tt=tt, tq=tq, bb=bb, tr=tr, hp=2)


def kernel(x_prompt, x_sample, cache_k, cache_v, cache_logf, state_hgrn, page_table, w_in, b_fox_f,
           hg_lower_bounds, hg_norm_w, w_out, norm_mix_pre, norm_mix_post, norm_ffn_pre, norm_ffn_post,
           w_gate, w_up, w_down):
    batch, seq, d = x_prompt.shape
    nseq, steps, _ = x_sample.shape
    depth, n_pool, page, fh, fd = cache_k.shape
    hh, hk, hv = state_hgrn.shape[2:]
    d_ff = w_gate.shape[2]
    assert hk == LANES and hv == LANES and fd == LANES and hh == fh
    heads = hh
    n_main = 7 * heads * LANES
    assert w_in.shape[2] == n_main + fh and fh <= LANES
    mp = batch * seq
    t = _tiles(batch, seq, nseq, steps, d_ff)
    assert seq % 16 == 0

    x = jnp.concatenate([x_prompt.reshape(mp, d), x_sample.reshape(nseq * steps, d)], axis=0)
    row = lambda a: a.reshape(1, -1).astype(F32)
    cache_logf_t = cache_logf.transpose(0, 1, 3, 2)
    w_in_b, w_out_b = w_in.astype(BF16), w_out.astype(BF16)
    w_gate_b, w_up_b, w_down_b = w_gate.astype(BF16), w_up.astype(BF16), w_down.astype(BF16)

    outs = {n: [] for n in ("gp", "sp", "ks", "vs", "gs")}
    hgrn_sample_states = kv_prompt = None
    for l in range(depth):
        w_ff = jnp.pad(w_in[l, :, n_main:], ((0, 0), (0, LANES - fh))).astype(BF16)
        b_ff = jnp.pad(b_fox_f[l].astype(F32), (0, LANES - fh)).reshape(1, LANES)
        z, logf, c = _inproj(x, row(norm_mix_pre[l]), w_in_b, w_ff, b_ff, layer=l, n=n_main, tm=t["tm"],
                             tn=t["tn"], seq_len=seq)

        gn = row(hg_norm_w[l])
        lb_raw = hg_lower_bounds.astype(F32)
        ohg_p, s_p = _hgrn_prompt(z, lb_raw, gn, batch=batch, seq=seq, heads=heads, layer=l,
                                  chunk=HGRN_CHUNK, tt=t["tt"], hp=t["hp"])
        ohg_s, hgrn_sample_states = _hgrn_sample(z, lb_raw, gn, state_hgrn, hgrn_sample_states, row0=mp,
                                                 nseq=nseq, steps=steps, heads=heads, layer=l, bb=t["bb"])

        c_t = c[:mp, :fh].reshape(batch, seq, fh).transpose(0, 2, 1).reshape(batch, fh, 1, seq)
        ofx_p = _fox_prompt(z, c_t, batch=batch, seq=seq, heads=heads, tq=t["tq"])
        ofx_s = _fox_decode(z, logf, cache_k, cache_v, cache_logf_t, page_table, row0=mp, layer=l, heads=heads,
                            steps=steps)

        hw = heads * LANES
        x1, h2 = _outproj(ohg_p, ohg_s, ofx_p, ofx_s, x, w_out_b, row(norm_mix_post[l]), row(norm_ffn_pre[l]),
                          layer=l, tm=t["tm_out"])
        x = _ffn(h2, x1, w_gate_b, w_up_b, w_down_b, row(norm_ffn_post[l]), layer=l,
                 tm=t["tm_ffn"], tf=t["tf"])

        kcol, vcol = 5 * heads * LANES, 6 * heads * LANES
        kv_prompt = _kv_rows(z, kv_prompt, depth=depth, rows=mp, heads=heads, layer=l, tr=t["tr"])
        outs["gp"].append(logf[:mp, :fh].reshape(batch, seq, fh))
        outs["sp"].append(s_p)
        outs["ks"].append(z[mp:, kcol:kcol + hw].reshape(nseq, steps, fh, fd))
        outs["vs"].append(z[mp:, vcol:vcol + hw].reshape(nseq, steps, fh, fd))
        outs["gs"].append(logf[mp:, :fh].reshape(nseq, steps, fh))

    st = {n: jnp.stack(v) for n, v in outs.items()}
    k_prompt, v_prompt = (a.reshape(depth, batch, seq, fh, fd) for a in kv_prompt)
    return (x[:mp].reshape(batch, seq, d), x[mp:].reshape(nseq, steps, d),
            k_prompt, v_prompt, st["gp"], st["sp"], st["ks"], st["vs"], st["gs"], hgrn_sample_states)
```

```python
import functools

import jax
import jax.numpy as jnp
from jax import lax
from jax.experimental import pallas as pl
from jax.experimental.pallas import tpu as pltpu

NORM_EPS = 1e-6
LOG2E = 1.4426950408889634
LANES = 128
SUBLANES = 8
VMEM_LIMIT_BYTES = 56 * 1024 * 1024
HGRN_CHUNK = 16
HGRN_ATT_ROWS = 256

F32 = jnp.float32
BF16 = jnp.bfloat16

_NT = (((1,), (1,)), ((), ()))
_TN = (((0,), (0,)), ((), ()))


def _params(*sem):
    return pltpu.CompilerParams(dimension_semantics=sem, vmem_limit_bytes=VMEM_LIMIT_BYTES)


def _rms(x):
    return x * lax.rsqrt(jnp.mean(x * x, axis=-1, keepdims=True) + NORM_EPS)


def _sigmoid_parts(z):
    e = jnp.exp(-jnp.abs(z))
    r = 1.0 / (1.0 + e)
    pos = z >= 0
    return jnp.where(pos, r, e * r), jnp.where(pos, e * r, r), jnp.minimum(z, 0.0) - jnp.log1p(e)


def _cumsum_rows(x, period=None):
    n = x.shape[0]
    span = n if period is None else period
    row = lax.broadcasted_iota(jnp.int32, x.shape, 0)
    if period is not None:
        row = row % period
    s = 1
    while s < span:
        x = x + jnp.where(row >= s, pltpu.roll(x, s, axis=0), 0.0)
        s *= 2
    return x


def _cumsum_lanes(x):
    n = x.shape[-1]
    lane = lax.broadcasted_iota(jnp.int32, x.shape, x.ndim - 1)
    s = 1
    while s < n:
        x = x + jnp.where(lane >= s, pltpu.roll(x, s, axis=x.ndim - 1), 0.0)
        s *= 2
    return x


def _inproj_kernel(x_ref, nw_ref, w_ref, wff_ref, bf_ref, z_ref, logf_ref, c_ref,
                   h_scr, carry_scr, *, seq_tiles):
    i = pl.program_id(0)
    j = pl.program_id(1)

    @pl.when(j == 0)
    def _():
        hb = (_rms(x_ref[...]) * nw_ref[...]).astype(BF16)
        h_scr[...] = hb
        ff = jnp.dot(hb, wff_ref[...], preferred_element_type=F32) + bf_ref[...]
        _, _, lf = _sigmoid_parts(ff)
        logf_ref[...] = lf

        @pl.when(i % seq_tiles == 0)
        def _():
            carry_scr[...] = jnp.zeros_like(carry_scr)

        c = _cumsum_rows(lf) + carry_scr[...]
        c_ref[...] = c
        carry_scr[...] = c[c.shape[0] - 1:, :]

    z_ref[...] = jnp.dot(h_scr[...], w_ref[...], preferred_element_type=F32)


def _inproj(x, nw, w_in, w_ff, b_ff, *, layer, n, tm, tn, seq_len):
    m, d = x.shape
    kern = functools.partial(_inproj_kernel, seq_tiles=seq_len // tm)
    return pl.pallas_call(
        kern,
        out_shape=(jax.ShapeDtypeStruct((m, n), F32),
                   jax.ShapeDtypeStruct((m, LANES), F32),
                   jax.ShapeDtypeStruct((m, LANES), F32)),
        grid=(m // tm, n // tn),
        in_specs=[pl.BlockSpec((tm, d), lambda i, j: (i, 0)),
                  pl.BlockSpec((1, d), lambda i, j: (0, 0)),
                  pl.BlockSpec((None, d, tn), lambda i, j: (layer, 0, j)),
                  pl.BlockSpec((d, LANES), lambda i, j: (0, 0)),
                  pl.BlockSpec((1, LANES), lambda i, j: (0, 0))],
        out_specs=(pl.BlockSpec((tm, tn), lambda i, j: (i, j)),
                   pl.BlockSpec((tm, LANES), lambda i, j: (i, 0)),
                   pl.BlockSpec((tm, LANES), lambda i, j: (i, 0))),
        scratch_shapes=[pltpu.VMEM((tm, d), BF16), pltpu.VMEM((1, LANES), F32)],
        compiler_params=_params("arbitrary", "arbitrary"),
        name="inproj",
    )(x, nw, w_in, w_ff, b_ff)


def _lower_bound(raw, layer):
    depth = raw.shape[0]
    rows = [raw[i:i + 1, :] for i in range(depth)]
    mx = functools.reduce(jnp.maximum, rows)
    ex = [jnp.exp(r - mx) for r in rows]
    tot = functools.reduce(lambda a, b: a + b, ex)
    p = [e / tot for e in ex]
    cum = functools.reduce(lambda a, b: a + b, p[:layer + 1])
    return cum - p[0]


def _hgrn_gates(lb, qr, z, chunk):
    sig_q, _, _ = _sigmoid_parts(qr)
    q = qr * sig_q
    _, sig_nz, logsig = _sigmoid_parts(z)
    a = jnp.log(lb)
    bb = jnp.log1p(-lb) + logsig
    logf = jnp.maximum(a, bb) + jnp.log1p(jnp.exp(-jnp.abs(a - bb)))
    k = (1.0 - lb) * sig_nz
    return q, k, _cumsum_rows(logf, period=chunk)


def _hgrn_tile(q, k, v, b, chunk):
    rows = q.shape[0]
    chunks = [slice(c * chunk, (c + 1) * chunk) for c in range(rows // chunk)]
    spread = lambda r: jnp.broadcast_to(r, (chunk, r.shape[1]))
    ref = jnp.concatenate([spread(b[s.start + chunk // 2 - 1:s.start + chunk // 2]) for s in chunks], axis=0)
    ends = [b[s.stop - 1:s.stop] for s in chunks]
    end = jnp.concatenate([spread(e) for e in ends], axis=0)
    vb = v.astype(BF16)
    qi = (q * jnp.exp(b - ref)).astype(BF16)
    ki = (k * jnp.exp(ref - b)).astype(BF16)
    g = min(rows, HGRN_ATT_ROWS)
    r = lax.broadcasted_iota(jnp.int32, (g, g), 0)
    c = lax.broadcasted_iota(jnp.int32, (g, g), 1)
    keep = (jnp.bitwise_and(r, -chunk) == jnp.bitwise_and(c, -chunk)) & (r >= c)
    parts = []
    for s in (slice(i, i + g) for i in range(0, rows, g)):
        att = lax.dot_general(qi[s], ki[s], _NT, preferred_element_type=F32)
        parts.append(jnp.dot(jnp.where(keep, att, 0.0).astype(BF16), vb[s], preferred_element_type=F32))
    intra = jnp.concatenate(parts, axis=0)
    qs = q * jnp.exp(b)
    ks = k * jnp.exp(end - b)
    incs = [lax.dot_general(v[s].astype(BF16), ks[s].astype(BF16), _TN, preferred_element_type=F32)
            for s in chunks]
    return [(s, intra[s], qs[s].astype(BF16), jnp.exp(e), u) for s, e, u in zip(chunks, ends, incs)]


def _hgrn_finish(o, g, gn):
    sig_g, _, _ = _sigmoid_parts(g)
    return (_rms(o) * gn * (g * sig_g)).astype(BF16)


def _hgrn_prompt_kernel(lb_ref, q_ref, f_ref, i_ref, g_ref, gn_ref, o_ref, sfin_ref,
                        st_scr, o_scr, *, chunk, layer):
    t = pl.program_id(2)

    @pl.when(t == 0)
    def _():
        st_scr[...] = jnp.zeros_like(st_scr)

    finals = []
    for j in range(st_scr.shape[0]):
        cols = slice(j * LANES, (j + 1) * LANES)
        lb = _lower_bound(lb_ref[:, cols], layer)
        q, k, b = _hgrn_gates(lb, q_ref[:, cols], f_ref[:, cols], chunk)
        st = st_scr[j]
        for sl, intra, qs, decay, inc in _hgrn_tile(q, k, i_ref[:, cols], b, chunk):
            o_scr[sl, cols] = intra + lax.dot_general(qs, st.astype(BF16), _NT, preferred_element_type=F32)
            st = st * decay + inc
        st_scr[j] = st
        finals.append(st)
        o_ref[:, cols] = _hgrn_finish(o_scr[:, cols], g_ref[:, cols], gn_ref[...])

    @pl.when(t == pl.num_programs(2) - 1)
    def _():
        for j, st in enumerate(finals):
            sfin_ref[j] = st.T


def _hgrn_prompt(z, lb_raw, gn, *, batch, seq, heads, layer, chunk, tt, hp):
    nt = seq // tt
    groups = heads // hp
    w = hp * LANES
    col = lambda part: (lambda b, h, t: (b * nt + t, part * groups + h))
    kern = functools.partial(_hgrn_prompt_kernel, chunk=chunk, layer=layer)
    depth = lb_raw.shape[0]
    return pl.pallas_call(
        kern,
        out_shape=(jax.ShapeDtypeStruct((batch * seq, heads * LANES), BF16),
                   jax.ShapeDtypeStruct((batch, heads, LANES, LANES), F32)),
        grid=(batch, groups, nt),
        in_specs=[pl.BlockSpec((depth, w), lambda b, h, t: (0, h)),
                  pl.BlockSpec((tt, w), col(0)),
                  pl.BlockSpec((tt, w), col(1)),
                  pl.BlockSpec((tt, w), col(2)),
                  pl.BlockSpec((tt, w), col(3)),
                  pl.BlockSpec((1, LANES), lambda b, h, t: (0, 0))],
        out_specs=(pl.BlockSpec((tt, w), lambda b, h, t: (b * nt + t, h)),
                   pl.BlockSpec((None, hp, LANES, LANES), lambda b, h, t: (b, h, 0, 0))),
        scratch_shapes=[pltpu.VMEM((hp, LANES, LANES), F32), pltpu.VMEM((tt, w), F32)],
        compiler_params=_params("arbitrary", "arbitrary", "arbitrary"),
        name="hgrn_prompt",
    )(lb_raw, z, z, z, z, gn)


def _hgrn_sample_kernel(lb_ref, q_ref, f_ref, i_ref, g_ref, gn_ref, s0_ref, *rest, chunk, layer):
    o_ref, sfin_ref, o_scr = rest[-3:]
    lb = _lower_bound(lb_ref[...], layer)
    q, k, b = _hgrn_gates(lb, q_ref[...], f_ref[...], chunk)
    for s, (sl, intra, qs, decay, inc) in enumerate(_hgrn_tile(q, k, i_ref[...], b, chunk)):
        st = s0_ref[s].T
        o_scr[sl, :] = intra + lax.dot_general(qs, st.astype(BF16), _NT, preferred_element_type=F32)
        sfin_ref[s] = (st * decay + inc).T
    o_ref[...] = _hgrn_finish(o_scr[...], g_ref[...], gn_ref[...])


def _hgrn_sample(z, lb_raw, gn, state, stacked, *, row0, nseq, steps, heads, layer, bb):
    rows = bb * steps
    rb0 = row0 // rows
    col = lambda base: (lambda i, h: (rb0 + i, base + h))
    kern = functools.partial(_hgrn_sample_kernel, chunk=steps, layer=layer)
    depth = lb_raw.shape[0]
    state_spec = pl.BlockSpec((None, bb, None, LANES, LANES), lambda i, h: (layer, i, h, 0, 0))
    in_specs = [pl.BlockSpec((depth, LANES), lambda i, h: (0, h)),
                pl.BlockSpec((rows, LANES), col(0)),
                pl.BlockSpec((rows, LANES), col(heads)),
                pl.BlockSpec((rows, LANES), col(2 * heads)),
                pl.BlockSpec((rows, LANES), col(3 * heads)),
                pl.BlockSpec((1, LANES), lambda i, h: (0, 0)),
                state_spec]
    args = [lb_raw, z, z, z, z, gn, state]
    aliases = {}
    if stacked is not None:
        in_specs.append(pl.BlockSpec(memory_space=pl.ANY))
        aliases = {len(args): 1}
        args.append(stacked)
    return pl.pallas_call(
        kern,
        out_shape=(jax.ShapeDtypeStruct((nseq * steps, heads * LANES), BF16),
                   jax.ShapeDtypeStruct(state.shape, F32)),
        grid=(nseq // bb, heads),
        in_specs=in_specs,
        out_specs=(pl.BlockSpec((rows, LANES), lambda i, h: (i, h)), state_spec),
        scratch_shapes=[pltpu.VMEM((rows, LANES), F32)],
        input_output_aliases=aliases,
        compiler_params=_params("arbitrary", "arbitrary"),
        name="hgrn_sample",
    )(*args)


def _fox_prompt_kernel(q_ref, k_ref, v_ref, c_ref, o_ref, vt_scr, *, tq, scale):
    h = pl.program_id(1)
    qi = pl.program_id(2)

    @pl.when(qi == 0)
    def _():
        vt_scr[...] = v_ref[...].T.astype(BF16)

    q_t = (q_ref[...] * (scale * LOG2E)).T.astype(BF16)
    lane = lax.broadcasted_iota(jnp.int32, (tq, LANES), 1)

    def rows(j):
        return pl.ds(pl.multiple_of(j * tq, tq), tq)

    def scores(j):
        kj = k_ref[rows(j), :].astype(BF16)
        c_keys = jnp.sum(jnp.where(lane == h, c_ref[rows(j), :], 0.0), axis=-1, keepdims=True)
        return jnp.dot(kj, q_t, preferred_element_type=F32) - c_keys * LOG2E

    def update(j, s, m, l, acc):
        m_new = jnp.maximum(m, jnp.max(s, axis=0, keepdims=True))
        alpha = jnp.exp2(m - m_new)
        p = jnp.exp2(s - m_new)
        l = alpha * l + jnp.sum(p, axis=0, keepdims=True)
        acc = alpha * acc + jnp.dot(vt_scr[:, rows(j)], p.astype(BF16), preferred_element_type=F32)
        return m_new, l, acc

    init = (jnp.full((1, tq), -jnp.inf, F32), jnp.zeros((1, tq), F32), jnp.zeros((LANES, tq), F32))
    stats = lax.fori_loop(0, qi, lambda j, st: update(j, scores(j), *st), init)
    causal = (lax.broadcasted_iota(jnp.int32, (tq, tq), 0) <= lax.broadcasted_iota(jnp.int32, (tq, tq), 1))
    _, l, acc = update(qi, jnp.where(causal, scores(qi), -jnp.inf), *stats)
    o_ref[...] = (acc / l).T.astype(BF16)


def _fox_prompt(z, c, *, batch, seq, heads, tq):
    nq = seq // tq
    d = LANES
    qcol, kcol, vcol = 4 * heads, 5 * heads, 6 * heads
    kern = functools.partial(_fox_prompt_kernel, tq=tq, scale=d ** -0.5)
    return pl.pallas_call(
        kern,
        out_shape=jax.ShapeDtypeStruct((batch * seq, heads * d), BF16),
        grid=(batch, heads, nq),
        in_specs=[pl.BlockSpec((tq, d), lambda b, h, i: (b * nq + i, qcol + h)),
                  pl.BlockSpec((seq, d), lambda b, h, i: (b, kcol + h)),
                  pl.BlockSpec((seq, d), lambda b, h, i: (b, vcol + h)),
                  pl.BlockSpec((seq, LANES), lambda b, h, i: (b, 0))],
        out_specs=pl.BlockSpec((tq, d), lambda b, h, i: (b * nq + i, h)),
        scratch_shapes=[pltpu.VMEM((d, seq), BF16)],
        compiler_params=_params("arbitrary", "arbitrary", "arbitrary"),
        name="fox_prompt",
    )(z, z, z, c)


def _fox_decode_kernel(pt_ref, q_ref, kn_ref, vn_ref, lfn_ref, *rest, n_pages, heads, steps, page, scale):
    del pt_ref
    k_refs, v_refs, lf_refs = rest[:n_pages], rest[n_pages:2 * n_pages], rest[2 * n_pages:3 * n_pages]
    o_ref = rest[3 * n_pages]
    d = LANES
    q = q_ref[...] * scale

    lf_pages = [r[...] for r in lf_refs]
    pad = lambda a: jnp.concatenate([a, jnp.zeros((page - steps, a.shape[1]), a.dtype)], axis=0)
    lf_pages.append(pad(lfn_ref[...]).T[0:heads, :])
    c_pages, before = [], jnp.zeros((heads, 1), F32)
    for lf in lf_pages:
        c_pages.append(_cumsum_lanes(lf) + before)
        before = before + jnp.sum(lf, axis=-1, keepdims=True)

    qpos = lax.broadcasted_iota(jnp.int32, (steps, page), 0)
    kpos = lax.broadcasted_iota(jnp.int32, (steps, page), 1)
    causal = kpos <= qpos

    def scores(h):
        cols = slice(h * d, (h + 1) * d)
        qh = q[:, cols].astype(BF16)
        keys = [r[pl.ds(h, page, stride=heads), :] for r in k_refs] + [pad(kn_ref[:, cols])]
        s = [lax.dot_general(qh, k.astype(BF16), _NT, preferred_element_type=F32) - c[h:h + 1, :]
             for k, c in zip(keys, c_pages)]
        s[-1] = jnp.where(causal, s[-1], -jnp.inf)
        return s

    def attend(h, s):
        cols = slice(h * d, (h + 1) * d)
        vals = [r[pl.ds(h, page, stride=heads), :] for r in v_refs] + [pad(vn_ref[:, cols])]
        m = functools.reduce(jnp.maximum, s)
        m = jnp.max(m, axis=-1, keepdims=True)
        p = [jnp.exp(t - m) for t in s]
        l = jnp.sum(functools.reduce(lambda a, b: a + b, p), axis=-1, keepdims=True)
        acc = functools.reduce(lambda a, b: a + b,
                               [jnp.dot(t.astype(BF16), v.astype(BF16), preferred_element_type=F32)
                                for t, v in zip(p, vals)])
        o_ref[:, cols] = acc / l

    ahead = min(2, heads)
    pending = [scores(h) for h in range(ahead)]
    for h in range(heads):
        if h + ahead < heads:
            pending.append(scores(h + ahead))
        attend(h, pending.pop(0))


def _fox_decode(z, logf, cache_k, cache_v, cache_logf_t, page_table, *, row0, layer, heads, steps):
    nseq, n_pages = page_table.shape
    depth, n_pool, page = cache_k.shape[:3]
    assert page == LANES
    d = LANES
    w = heads * d
    rb0 = row0 // steps
    qcol, kcol, vcol = 4, 5, 6
    ck = cache_k.reshape(depth, n_pool, page * heads, d)
    cv = cache_v.reshape(depth, n_pool, page * heads, d)
    kern = functools.partial(_fox_decode_kernel, n_pages=n_pages, heads=heads, steps=steps, page=page,
                             scale=d ** -0.5)
    pg = lambda i: (lambda b, pt: (layer, pt[b, i], 0, 0))
    kv_spec = [pl.BlockSpec((None, None, page * heads, d), pg(i)) for i in range(n_pages)]
    lf_spec = [pl.BlockSpec((None, None, heads, page), pg(i)) for i in range(n_pages)]
    grid_spec = pltpu.PrefetchScalarGridSpec(
        num_scalar_prefetch=1,
        grid=(nseq,),
        in_specs=[pl.BlockSpec((steps, w), lambda b, pt: (rb0 + b, qcol)),
                  pl.BlockSpec((steps, w), lambda b, pt: (rb0 + b, kcol)),
                  pl.BlockSpec((steps, w), lambda b, pt: (rb0 + b, vcol)),
                  pl.BlockSpec((steps, LANES), lambda b, pt: (rb0 + b, 0))]
                 + kv_spec + kv_spec + lf_spec,
        out_specs=pl.BlockSpec((steps, w), lambda b, pt: (b, 0)),
    )
    return pl.pallas_call(
        kern,
        out_shape=jax.ShapeDtypeStruct((nseq * steps, w), F32),
        grid_spec=grid_spec,
        compiler_params=_params("arbitrary"),
        name="fox_decode",
    )(page_table, z, z, z, logf, *([ck] * n_pages), *([cv] * n_pages), *([cache_logf_t] * n_pages))


def _kv_rows_kernel(k_ref, v_ref, *rest, heads):
    ko_ref, vo_ref = rest[-2:]
    rows = k_ref.shape[0]
    for src, dst in ((k_ref, ko_ref), (v_ref, vo_ref)):
        for h in range(heads):
            dst[pl.ds(h, rows, stride=heads), :] = src[:, h * LANES:(h + 1) * LANES]


def _kv_rows(z, stacked, *, depth, rows, heads, layer, tr):
    w = heads * LANES
    kcol, vcol = 5, 6
    shape = jax.ShapeDtypeStruct((depth, rows * heads, LANES), F32)
    out_spec = pl.BlockSpec((None, tr * heads, LANES), lambda i: (layer, i, 0))
    in_specs = [pl.BlockSpec((tr, w), lambda i: (i, kcol)), pl.BlockSpec((tr, w), lambda i: (i, vcol))]
    args = [z, z]
    aliases = {}
    if stacked is not None:
        in_specs += [pl.BlockSpec(memory_space=pl.ANY)] * 2
        aliases = {2: 0, 3: 1}
        args += list(stacked)
    return pl.pallas_call(
        functools.partial(_kv_rows_kernel, heads=heads),
        out_shape=(shape, shape),
        grid=(rows // tr,),
        in_specs=in_specs,
        out_specs=(out_spec, out_spec),
        input_output_aliases=aliases,
        compiler_params=_params("arbitrary"),
        name="kv_rows",
    )(*args)


def _outproj_kernel(ohp_ref, ohs_ref, ofp_ref, ofs_ref, x_ref, wa_ref, wb_ref, npost_ref, npre_ref,
                    x1_ref, h2_ref, *, prompt_tiles):
    is_prompt = pl.program_id(0) < prompt_tiles
    o_hg = jnp.where(is_prompt, ohp_ref[...], ohs_ref[...])
    o_fx = jnp.where(is_prompt, ofp_ref[...], ofs_ref[...].astype(BF16))
    y = (jnp.dot(o_hg, wa_ref[...], preferred_element_type=F32)
         + jnp.dot(o_fx, wb_ref[...], preferred_element_type=F32))
    x1 = x_ref[...] + _rms(y) * npost_ref[...]
    x1_ref[...] = x1
    h2_ref[...] = (_rms(x1) * npre_ref[...]).astype(BF16)


def _outproj(ohg_p, ohg_s, ofx_p, ofx_s, x, w_out, n_post, n_pre, *, layer, tm):
    m, d = x.shape
    wd = ohg_p.shape[1]
    prompt_tiles = ohg_p.shape[0] // tm
    row = lambda i: (i, 0)
    prow = lambda i: (jnp.minimum(i, prompt_tiles - 1), 0)
    srow = lambda i: (jnp.maximum(i - prompt_tiles, 0), 0)
    fixed = lambda i: (0, 0)
    return pl.pallas_call(
        functools.partial(_outproj_kernel, prompt_tiles=prompt_tiles),
        out_shape=(jax.ShapeDtypeStruct((m, d), F32), jax.ShapeDtypeStruct((m, d), BF16)),
        grid=(m // tm,),
        in_specs=[pl.BlockSpec((tm, wd), prow), pl.BlockSpec((tm, wd), srow),
                  pl.BlockSpec((tm, wd), prow), pl.BlockSpec((tm, wd), srow),
                  pl.BlockSpec((tm, d), row),
                  pl.BlockSpec((None, wd, d), lambda i: (layer, 0, 0)),
                  pl.BlockSpec((None, wd, d), lambda i: (layer, 1, 0)),
                  pl.BlockSpec((1, d), fixed), pl.BlockSpec((1, d), fixed)],
        out_specs=(pl.BlockSpec((tm, d), row), pl.BlockSpec((tm, d), row)),
        compiler_params=_params("arbitrary"),
        name="outproj",
    )(ohg_p, ohg_s, ofx_p, ofx_s, x, w_out, w_out, n_post, n_pre)


def _ffn_kernel(h_ref, x_ref, wg_ref, wu_ref, wd_ref, npost_ref, o_ref):
    j = pl.program_id(1)

    @pl.when(j == 0)
    def _():
        o_ref[...] = jnp.zeros_like(o_ref)

    h = h_ref[...]
    a = jnp.dot(h, wg_ref[...], preferred_element_type=F32)
    b = jnp.dot(h, wu_ref[...], preferred_element_type=F32)
    sig_a, _, _ = _sigmoid_parts(a)
    o_ref[...] += jnp.dot((a * sig_a * b).astype(BF16), wd_ref[...], preferred_element_type=F32)

    @pl.when(j == pl.num_programs(1) - 1)
    def _():
        o_ref[...] = x_ref[...] + _rms(o_ref[...]) * npost_ref[...]


def _ffn(h2, x1, w_g, w_u, w_d, n_post, *, layer, tm, tf):
    m, d = x1.shape
    f = w_g.shape[2]
    return pl.pallas_call(
        _ffn_kernel,
        out_shape=jax.ShapeDtypeStruct((m, d), F32),
        grid=(m // tm, f // tf),
        in_specs=[pl.BlockSpec((tm, d), lambda i, j: (i, 0)),
                  pl.BlockSpec((tm, d), lambda i, j: (i, 0)),
                  pl.BlockSpec((None, d, tf), lambda i, j: (layer, 0, j)),
                  pl.BlockSpec((None, d, tf), lambda i, j: (layer, 0, j)),
                  pl.BlockSpec((None, tf, d), lambda i, j: (layer, j, 0)),
                  pl.BlockSpec((1, d), lambda i, j: (0, 0))],
        out_specs=pl.BlockSpec((tm, d), lambda i, j: (i, 0)),
        compiler_params=_params("arbitrary", "arbitrary"),
        name="ffn",
    )(h2, x1, w_g, w_u, w_d, n_post)


def _tiles(batch, seq, nseq, steps, d_ff):
    m = batch * seq + nseq * steps
    tm = next(t for t in (1024, 512, 256, 128) if seq % t == 0 and m % t == 0)
    tm_out = next(t for t in (512, 256, 128) if (batch * seq) % t == 0 and (nseq * steps) % t == 0)
    tm_ffn = next(t for t in (768, 512, 256, 128) if m % t == 0)
    tf = next(t for t in (512, 256, 128) if d_ff % t == 0)
    tt = next(t for t in (512, 256, 128, HGRN_CHUNK) if seq % t == 0)
    tq = next(t for t in (1024, 512, 256, 128) if seq % t == 0)
    bb = next(b for b in (16, 8, 4, 2, 1) if nseq % b == 0 and (batch * seq) % (b * steps) == 0)
    tr = next(t for t in (512, 256, 128) if (batch * seq) % t == 0)
    return dict(tm=tm, tn=1024, tm_out=tm_out, tm_ffn=tm_ffn, tf=tf, tt=tt, tq=tq, bb=bb, tr=tr, hp=4)


def kernel(x_prompt, x_sample, cache_k, cache_v, cache_logf, state_hgrn, page_table, w_in, b_fox_f,
           hg_lower_bounds, hg_norm_w, w_out, norm_mix_pre, norm_mix_post, norm_ffn_pre, norm_ffn_post,
           w_gate, w_up, w_down):
    batch, seq, d = x_prompt.shape
    nseq, steps, _ = x_sample.shape
    depth, n_pool, page, fh, fd = cache_k.shape
    hh, hk, hv = state_hgrn.shape[2:]
    d_ff = w_gate.shape[2]
    assert hk == LANES and hv == LANES and fd == LANES and hh == fh
    heads = hh
    n_main = 7 * heads * LANES
    assert w_in.shape[2] == n_main + fh and fh <= LANES
    mp = batch * seq
    t = _tiles(batch, seq, nseq, steps, d_ff)
    assert seq % 16 == 0

    x = jnp.concatenate([x_prompt.reshape(mp, d), x_sample.reshape(nseq * steps, d)], axis=0)
    row = lambda a: a.reshape(1, -1).astype(F32)
    cache_logf_t = cache_logf.transpose(0, 1, 3, 2)
    w_in_b, w_out_b = w_in.astype(BF16), w_out.astype(BF16)
    w_gate_b, w_up_b, w_down_b = w_gate.astype(BF16), w_up.astype(BF16), w_down.astype(BF16)

    outs = {n: [] for n in ("gp", "sp", "ks", "vs", "gs")}
    hgrn_sample_states = kv_prompt = None
    for l in range(depth):
        w_ff = jnp.pad(w_in[l, :, n_main:], ((0, 0), (0, LANES - fh))).astype(BF16)
        b_ff = jnp.pad(b_fox_f[l].astype(F32), (0, LANES - fh)).reshape(1, LANES)
        z, logf, c = _inproj(x, row(norm_mix_pre[l]), w_in_b, w_ff, b_ff, layer=l, n=n_main, tm=t["tm"],
                             tn=t["tn"], seq_len=seq)

        gn = row(hg_norm_w[l])
        lb_raw = hg_lower_bounds.astype(F32)
        ohg_p, s_p = _hgrn_prompt(z, lb_raw, gn, batch=batch, seq=seq, heads=heads, layer=l,
                                  chunk=HGRN_CHUNK, tt=t["tt"], hp=t["hp"])
        ohg_s, hgrn_sample_states = _hgrn_sample(z, lb_raw, gn, state_hgrn, hgrn_sample_states, row0=mp,
                                                 nseq=nseq, steps=steps, heads=heads, layer=l, bb=t["bb"])

        ofx_p = _fox_prompt(z, c, batch=batch, seq=seq, heads=heads, tq=t["tq"])
        ofx_s = _fox_decode(z, logf, cache_k, cache_v, cache_logf_t, page_table, row0=mp, layer=l, heads=heads,
                            steps=steps)

        hw = heads * LANES
        x1, h2 = _outproj(ohg_p, ohg_s, ofx_p, ofx_s, x, w_out_b, row(norm_mix_post[l]), row(norm_ffn_pre[l]),
                          layer=l, tm=t["tm_out"])
        x = _ffn(h2, x1, w_gate_b, w_up_b, w_down_b, row(norm_ffn_post[l]), layer=l,
                 tm=t["tm_ffn"], tf=t["tf"])

        kcol, vcol = 5 * heads * LANES, 6 * heads * LANES
        kv_prompt = _kv_rows(z, kv_prompt, depth=depth, rows=mp, heads=heads, layer=l, tr=t["tr"])
        outs["gp"].append(logf[:mp, :fh].reshape(batch, seq, fh))
        outs["sp"].append(s_p)
        outs["ks"].append(z[mp:, kcol:kcol + hw].reshape(nseq, steps, fh, fd))
        outs["vs"].append(z[mp:, vcol:vcol + hw].reshape(nseq, steps, fh, fd))
        outs["gs"].append(logf[mp:, :fh].reshape(nseq, steps, fh))

    st = {n: jnp.stack(v) for n, v in outs.items()}
    k_prompt, v_prompt = (a.reshape(depth, batch, seq, fh, fd) for a in kv_prompt)
    return (x[:mp].reshape(batch, seq, d), x[mp:].reshape(nseq, steps, d),
            k_prompt, v_prompt, st["gp"], st["sp"], st["ks"], st["vs"], st["gs"], hgrn_sample_states)
```

```python
import functools

import jax
import jax.numpy as jnp
from jax import lax
from jax.experimental import pallas as pl
from jax.experimental.pallas import tpu as pltpu

NORM_EPS = 1e-6
LOG2E = 1.4426950408889634
LANES = 128
SUBLANES = 8
VMEM_LIMIT_BYTES = 56 * 1024 * 1024
HGRN_CHUNK = 16
HGRN_ATT_ROWS = 256

F32 = jnp.float32
BF16 = jnp.bfloat16

_NT = (((1,), (1,)), ((), ()))
_TN = (((0,), (0,)), ((), ()))


def _params(*sem):
    return pltpu.CompilerParams(dimension_semantics=sem, vmem_limit_bytes=VMEM_LIMIT_BYTES)


def _rms(x):
    return x * lax.rsqrt(jnp.mean(x * x, axis=-1, keepdims=True) + NORM_EPS)


def _sigmoid_parts(z):
    e = jnp.exp(-jnp.abs(z))
    r = 1.0 / (1.0 + e)
    pos = z >= 0
    return jnp.where(pos, r, e * r), jnp.where(pos, e * r, r), jnp.minimum(z, 0.0) - jnp.log1p(e)


def _cumsum_rows(x, period=None):
    n = x.shape[0]
    span = n if period is None else period
    row = lax.broadcasted_iota(jnp.int32, x.shape, 0)
    if period is not None:
        row = row % period
    s = 1
    while s < span:
        x = x + jnp.where(row >= s, pltpu.roll(x, s, axis=0), 0.0)
        s *= 2
    return x


def _cumsum_lanes(x):
    n = x.shape[-1]
    lane = lax.broadcasted_iota(jnp.int32, x.shape, x.ndim - 1)
    s = 1
    while s < n:
        x = x + jnp.where(lane >= s, pltpu.roll(x, s, axis=x.ndim - 1), 0.0)
        s *= 2
    return x


def _inproj_kernel(x_ref, nw_ref, w_ref, wff_ref, bf_ref, z_ref, logf_ref, c_ref,
                   h_scr, carry_scr, *, seq_tiles):
    i = pl.program_id(0)
    j = pl.program_id(1)

    @pl.when(j == 0)
    def _():
        hb = (_rms(x_ref[...]) * nw_ref[...]).astype(BF16)
        h_scr[...] = hb
        ff = jnp.dot(hb, wff_ref[...], preferred_element_type=F32) + bf_ref[...]
        _, _, lf = _sigmoid_parts(ff)
        logf_ref[...] = lf

        @pl.when(i % seq_tiles == 0)
        def _():
            carry_scr[...] = jnp.zeros_like(carry_scr)

        c = _cumsum_rows(lf) + carry_scr[...]
        c_ref[...] = c
        carry_scr[...] = c[c.shape[0] - 1:, :]

    z_ref[...] = jnp.dot(h_scr[...], w_ref[...], preferred_element_type=F32)


def _inproj(x, nw, w_in, w_ff, b_ff, *, layer, n, tm, tn, seq_len):
    m, d = x.shape
    kern = functools.partial(_inproj_kernel, seq_tiles=seq_len // tm)
    return pl.pallas_call(
        kern,
        out_shape=(jax.ShapeDtypeStruct((m, n), F32),
                   jax.ShapeDtypeStruct((m, LANES), F32),
                   jax.ShapeDtypeStruct((m, LANES), F32)),
        grid=(m // tm, n // tn),
        in_specs=[pl.BlockSpec((tm, d), lambda i, j: (i, 0)),
                  pl.BlockSpec((1, d), lambda i, j: (0, 0)),
                  pl.BlockSpec((None, d, tn), lambda i, j: (layer, 0, j)),
                  pl.BlockSpec((d, LANES), lambda i, j: (0, 0)),
                  pl.BlockSpec((1, LANES), lambda i, j: (0, 0))],
        out_specs=(pl.BlockSpec((tm, tn), lambda i, j: (i, j)),
                   pl.BlockSpec((tm, LANES), lambda i, j: (i, 0)),
                   pl.BlockSpec((tm, LANES), lambda i, j: (i, 0))),
        scratch_shapes=[pltpu.VMEM((tm, d), BF16), pltpu.VMEM((1, LANES), F32)],
        compiler_params=_params("arbitrary", "arbitrary"),
        name="inproj",
    )(x, nw, w_in, w_ff, b_ff)


def _lower_bound(raw, layer):
    depth = raw.shape[0]
    rows = [raw[i:i + 1, :] for i in range(depth)]
    mx = functools.reduce(jnp.maximum, rows)
    ex = [jnp.exp(r - mx) for r in rows]
    tot = functools.reduce(lambda a, b: a + b, ex)
    p = [e / tot for e in ex]
    cum = functools.reduce(lambda a, b: a + b, p[:layer + 1])
    return cum - p[0]


def _hgrn_gates(lb, qr, z, chunk):
    sig_q, _, _ = _sigmoid_parts(qr)
    q = qr * sig_q
    _, sig_nz, logsig = _sigmoid_parts(z)
    a = jnp.log(lb)
    bb = jnp.log1p(-lb) + logsig
    logf = jnp.maximum(a, bb) + jnp.log1p(jnp.exp(-jnp.abs(a - bb)))
    k = (1.0 - lb) * sig_nz
    return q, k, _cumsum_rows(logf, period=chunk)


def _hgrn_tile(q, k, v, b, chunk):
    rows = q.shape[0]
    chunks = [slice(c * chunk, (c + 1) * chunk) for c in range(rows // chunk)]
    spread = lambda r: jnp.broadcast_to(r, (chunk, r.shape[1]))
    ref = jnp.concatenate([spread(b[s.start + chunk // 2 - 1:s.start + chunk // 2]) for s in chunks], axis=0)
    ends = [b[s.stop - 1:s.stop] for s in chunks]
    end = jnp.concatenate([spread(e) for e in ends], axis=0)
    vb = v.astype(BF16)
    qi = (q * jnp.exp(b - ref)).astype(BF16)
    ki = (k * jnp.exp(ref - b)).astype(BF16)
    g = min(rows, HGRN_ATT_ROWS)
    r = lax.broadcasted_iota(jnp.int32, (g, g), 0)
    c = lax.broadcasted_iota(jnp.int32, (g, g), 1)
    keep = (jnp.bitwise_and(r, -chunk) == jnp.bitwise_and(c, -chunk)) & (r >= c)
    parts = []
    for s in (slice(i, i + g) for i in range(0, rows, g)):
        att = lax.dot_general(qi[s], ki[s], _NT, preferred_element_type=F32)
        parts.append(jnp.dot(jnp.where(keep, att, 0.0).astype(BF16), vb[s], preferred_element_type=F32))
    intra = jnp.concatenate(parts, axis=0)
    qs = q * jnp.exp(b)
    ks = k * jnp.exp(end - b)
    incs = [lax.dot_general(v[s].astype(BF16), ks[s].astype(BF16), _TN, preferred_element_type=F32)
            for s in chunks]
    return [(s, intra[s], qs[s].astype(BF16), jnp.exp(e), u) for s, e, u in zip(chunks, ends, incs)]


def _hgrn_finish(o, g, gn):
    sig_g, _, _ = _sigmoid_parts(g)
    return (_rms(o) * gn * (g * sig_g)).astype(BF16)


def _hgrn_prompt_kernel(lb_ref, q_ref, f_ref, i_ref, g_ref, gn_ref, o_ref, sfin_ref,
                        st_scr, o_scr, *, chunk, layer):
    t = pl.program_id(2)

    @pl.when(t == 0)
    def _():
        st_scr[...] = jnp.zeros_like(st_scr)

    finals = []
    for j in range(st_scr.shape[0]):
        cols = slice(j * LANES, (j + 1) * LANES)
        lb = _lower_bound(lb_ref[:, cols], layer)
        q, k, b = _hgrn_gates(lb, q_ref[:, cols], f_ref[:, cols], chunk)
        st = st_scr[j]
        for sl, intra, qs, decay, inc in _hgrn_tile(q, k, i_ref[:, cols], b, chunk):
            o_scr[sl, cols] = intra + lax.dot_general(qs, st.astype(BF16), _NT, preferred_element_type=F32)
            st = st * decay + inc
        st_scr[j] = st
        finals.append(st)
        o_ref[:, cols] = _hgrn_finish(o_scr[:, cols], g_ref[:, cols], gn_ref[...])

    @pl.when(t == pl.num_programs(2) - 1)
    def _():
        for j, st in enumerate(finals):
            sfin_ref[j] = st.T


def _hgrn_prompt(z, lb_raw, gn, *, batch, seq, heads, layer, chunk, tt, hp):
    nt = seq // tt
    groups = heads // hp
    w = hp * LANES
    col = lambda part: (lambda b, h, t: (b * nt + t, part * groups + h))
    kern = functools.partial(_hgrn_prompt_kernel, chunk=chunk, layer=layer)
    depth = lb_raw.shape[0]
    return pl.pallas_call(
        kern,
        out_shape=(jax.ShapeDtypeStruct((batch * seq, heads * LANES), BF16),
                   jax.ShapeDtypeStruct((batch, heads, LANES, LANES), F32)),
        grid=(batch, groups, nt),
        in_specs=[pl.BlockSpec((depth, w), lambda b, h, t: (0, h)),
                  pl.BlockSpec((tt, w), col(0)),
                  pl.BlockSpec((tt, w), col(1)),
                  pl.BlockSpec((tt, w), col(2)),
                  pl.BlockSpec((tt, w), col(3)),
                  pl.BlockSpec((1, LANES), lambda b, h, t: (0, 0))],
        out_specs=(pl.BlockSpec((tt, w), lambda b, h, t: (b * nt + t, h)),
                   pl.BlockSpec((None, hp, LANES, LANES), lambda b, h, t: (b, h, 0, 0))),
        scratch_shapes=[pltpu.VMEM((hp, LANES, LANES), F32), pltpu.VMEM((tt, w), F32)],
        compiler_params=_params("arbitrary", "arbitrary", "arbitrary"),
        name="hgrn_prompt",
    )(lb_raw, z, z, z, z, gn)


def _hgrn_sample_kernel(lb_ref, q_ref, f_ref, i_ref, g_ref, gn_ref, s0_ref, *rest, chunk, layer):
    o_ref, sfin_ref, o_scr = rest[-3:]
    lb = _lower_bound(lb_ref[...], layer)
    q, k, b = _hgrn_gates(lb, q_ref[...], f_ref[...], chunk)
    for s, (sl, intra, qs, decay, inc) in enumerate(_hgrn_tile(q, k, i_ref[...], b, chunk)):
        st = s0_ref[s].T
        o_scr[sl, :] = intra + lax.dot_general(qs, st.astype(BF16), _NT, preferred_element_type=F32)
        sfin_ref[s] = (st * decay + inc).T
    o_ref[...] = _hgrn_finish(o_scr[...], g_ref[...], gn_ref[...])


def _hgrn_sample(z, lb_raw, gn, state, stacked, *, row0, nseq, steps, heads, layer, bb):
    rows = bb * steps
    rb0 = row0 // rows
    col = lambda base: (lambda i, h: (rb0 + i, base + h))
    kern = functools.partial(_hgrn_sample_kernel, chunk=steps, layer=layer)
    depth = lb_raw.shape[0]
    state_spec = pl.BlockSpec((None, bb, None, LANES, LANES), lambda i, h: (layer, i, h, 0, 0))
    in_specs = [pl.BlockSpec((depth, LANES), lambda i, h: (0, h)),
                pl.BlockSpec((rows, LANES), col(0)),
                pl.BlockSpec((rows, LANES), col(heads)),
                pl.BlockSpec((rows, LANES), col(2 * heads)),
                pl.BlockSpec((rows, LANES), col(3 * heads)),
                pl.BlockSpec((1, LANES), lambda i, h: (0, 0)),
                state_spec]
    args = [lb_raw, z, z, z, z, gn, state]
    aliases = {}
    if stacked is not None:
        in_specs.append(pl.BlockSpec(memory_space=pl.ANY))
        aliases = {len(args): 1}
        args.append(stacked)
    return pl.pallas_call(
        kern,
        out_shape=(jax.ShapeDtypeStruct((nseq * steps, heads * LANES), BF16),
                   jax.ShapeDtypeStruct(state.shape, F32)),
        grid=(nseq // bb, heads),
        in_specs=in_specs,
        out_specs=(pl.BlockSpec((rows, LANES), lambda i, h: (i, h)), state_spec),
        scratch_shapes=[pltpu.VMEM((rows, LANES), F32)],
        input_output_aliases=aliases,
        compiler_params=_params("arbitrary", "arbitrary"),
        name="hgrn_sample",
    )(*args)


def _fox_prompt_kernel(q_ref, k_ref, v_ref, c_ref, o_ref, vt_scr, *, tq, scale):
    h = pl.program_id(1)
    qi = pl.program_id(2)

    @pl.when(qi == 0)
    def _():
        vt_scr[...] = v_ref[...].T.astype(BF16)

    q_t = (q_ref[...] * (scale * LOG2E)).T.astype(BF16)
    lane = lax.broadcasted_iota(jnp.int32, (tq, LANES), 1)

    def rows(j):
        return pl.ds(pl.multiple_of(j * tq, tq), tq)

    def scores(j):
        kj = k_ref[rows(j), :].astype(BF16)
        c_keys = jnp.sum(jnp.where(lane == h, c_ref[rows(j), :], 0.0), axis=-1, keepdims=True)
        return jnp.dot(kj, q_t, preferred_element_type=F32) - c_keys * LOG2E

    def update(j, s, m, l, acc):
        m_new = jnp.maximum(m, jnp.max(s, axis=0, keepdims=True))
        alpha = jnp.exp2(m - m_new)
        p = jnp.exp2(s - m_new)
        l = alpha * l + jnp.sum(p, axis=0, keepdims=True)
        acc = alpha * acc + jnp.dot(vt_scr[:, rows(j)], p.astype(BF16), preferred_element_type=F32)
        return m_new, l, acc

    init = (jnp.full((1, tq), -jnp.inf, F32), jnp.zeros((1, tq), F32), jnp.zeros((LANES, tq), F32))
    stats = lax.fori_loop(0, qi, lambda j, st: update(j, scores(j), *st), init)
    causal = (lax.broadcasted_iota(jnp.int32, (tq, tq), 0) <= lax.broadcasted_iota(jnp.int32, (tq, tq), 1))
    _, l, acc = update(qi, jnp.where(causal, scores(qi), -jnp.inf), *stats)
    o_ref[...] = (acc / l).T.astype(BF16)


def _fox_prompt(z, c, *, batch, seq, heads, tq):
    nq = seq // tq
    d = LANES
    qcol, kcol, vcol = 4 * heads, 5 * heads, 6 * heads
    kern = functools.partial(_fox_prompt_kernel, tq=tq, scale=d ** -0.5)
    return pl.pallas_call(
        kern,
        out_shape=jax.ShapeDtypeStruct((batch * seq, heads * d), BF16),
        grid=(batch, heads, nq),
        in_specs=[pl.BlockSpec((tq, d), lambda b, h, i: (b * nq + i, qcol + h)),
                  pl.BlockSpec((seq, d), lambda b, h, i: (b, kcol + h)),
                  pl.BlockSpec((seq, d), lambda b, h, i: (b, vcol + h)),
                  pl.BlockSpec((seq, LANES), lambda b, h, i: (b, 0))],
        out_specs=pl.BlockSpec((tq, d), lambda b, h, i: (b * nq + i, h)),
        scratch_shapes=[pltpu.VMEM((d, seq), BF16)],
        compiler_params=_params("arbitrary", "arbitrary", "arbitrary"),
        name="fox_prompt",
    )(z, z, z, c)


def _fox_decode_kernel(pt_ref, q_ref, kn_ref, vn_ref, lfn_ref, *rest, n_pages, heads, steps, page, scale):
    k_refs, v_refs = rest[:n_pages], rest[n_pages:2 * n_pages]
    lf_ref, o_ref = rest[2 * n_pages:]
    d = LANES
    q = q_ref[...] * scale

    seq_id = pl.program_id(0)
    lf_pages = [lf_ref[pt_ref[seq_id, i]] for i in range(n_pages)]
    pad = lambda a: jnp.concatenate([a, jnp.zeros((page - steps, a.shape[1]), a.dtype)], axis=0)
    lf_pages.append(pad(lfn_ref[...]).T[0:heads, :])
    c_pages, before = [], jnp.zeros((heads, 1), F32)
    for lf in lf_pages:
        c_pages.append(_cumsum_lanes(lf) + before)
        before = before + jnp.sum(lf, axis=-1, keepdims=True)

    qpos = lax.broadcasted_iota(jnp.int32, (steps, page), 0)
    kpos = lax.broadcasted_iota(jnp.int32, (steps, page), 1)
    causal = kpos <= qpos

    def scores(h):
        cols = slice(h * d, (h + 1) * d)
        qh = q[:, cols].astype(BF16)
        keys = [r[pl.ds(h, page, stride=heads), :] for r in k_refs] + [pad(kn_ref[:, cols])]
        s = [lax.dot_general(qh, k.astype(BF16), _NT, preferred_element_type=F32) - c[h:h + 1, :]
             for k, c in zip(keys, c_pages)]
        s[-1] = jnp.where(causal, s[-1], -jnp.inf)
        return s

    def attend(h, s):
        cols = slice(h * d, (h + 1) * d)
        vals = [r[pl.ds(h, page, stride=heads), :] for r in v_refs] + [pad(vn_ref[:, cols])]
        m = functools.reduce(jnp.maximum, s)
        m = jnp.max(m, axis=-1, keepdims=True)
        p = [jnp.exp(t - m) for t in s]
        l = jnp.sum(functools.reduce(lambda a, b: a + b, p), axis=-1, keepdims=True)
        acc = functools.reduce(lambda a, b: a + b,
                               [jnp.dot(t.astype(BF16), v.astype(BF16), preferred_element_type=F32)
                                for t, v in zip(p, vals)])
        o_ref[:, cols] = acc / l

    ahead = min(2, heads)
    pending = [scores(h) for h in range(ahead)]
    for h in range(heads):
        if h + ahead < heads:
            pending.append(scores(h + ahead))
        attend(h, pending.pop(0))


def _fox_decode(z, logf, cache_k, cache_v, cache_logf_t, page_table, *, row0, layer, heads, steps):
    nseq, n_pages = page_table.shape
    depth, n_pool, page = cache_k.shape[:3]
    assert page == LANES
    d = LANES
    w = heads * d
    rb0 = row0 // steps
    qcol, kcol, vcol = 4, 5, 6
    ck = cache_k.reshape(depth, n_pool, page * heads, d)
    cv = cache_v.reshape(depth, n_pool, page * heads, d)
    kern = functools.partial(_fox_decode_kernel, n_pages=n_pages, heads=heads, steps=steps, page=page,
                             scale=d ** -0.5)
    pg = lambda i: (lambda b, pt: (layer, pt[b, i], 0, 0))
    kv_spec = [pl.BlockSpec((None, None, page * heads, d), pg(i)) for i in range(n_pages)]
    lf_spec = pl.BlockSpec((None, n_pool, heads, page), lambda b, pt: (layer, 0, 0, 0),
                           pipeline_mode=pl.Buffered(1))
    grid_spec = pltpu.PrefetchScalarGridSpec(
        num_scalar_prefetch=1,
        grid=(nseq,),
        in_specs=[pl.BlockSpec((steps, w), lambda b, pt: (rb0 + b, qcol)),
                  pl.BlockSpec((steps, w), lambda b, pt: (rb0 + b, kcol)),
                  pl.BlockSpec((steps, w), lambda b, pt: (rb0 + b, vcol)),
                  pl.BlockSpec((steps, LANES), lambda b, pt: (rb0 + b, 0))]
                 + kv_spec + kv_spec + [lf_spec],
        out_specs=pl.BlockSpec((steps, w), lambda b, pt: (b, 0)),
    )
    return pl.pallas_call(
        kern,
        out_shape=jax.ShapeDtypeStruct((nseq * steps, w), F32),
        grid_spec=grid_spec,
        compiler_params=_params("arbitrary"),
        name="fox_decode",
    )(page_table, z, z, z, logf, *([ck] * n_pages), *([cv] * n_pages), cache_logf_t)


def _kv_rows_kernel(k_ref, v_ref, *rest, heads):
    ko_ref, vo_ref = rest[-2:]
    rows = k_ref.shape[0]
    for src, dst in ((k_ref, ko_ref), (v_ref, vo_ref)):
        for h in range(heads):
            dst[pl.ds(h, rows, stride=heads), :] = src[:, h * LANES:(h + 1) * LANES]


def _kv_rows(z, stacked, *, depth, rows, heads, layer, tr):
    w = heads * LANES
    kcol, vcol = 5, 6
    shape = jax.ShapeDtypeStruct((depth, rows * heads, LANES), F32)
    out_spec = pl.BlockSpec((None, tr * heads, LANES), lambda i: (layer, i, 0))
    in_specs = [pl.BlockSpec((tr, w), lambda i: (i, kcol)), pl.BlockSpec((tr, w), lambda i: (i, vcol))]
    args = [z, z]
    aliases = {}
    if stacked is not None:
        in_specs += [pl.BlockSpec(memory_space=pl.ANY)] * 2
        aliases = {2: 0, 3: 1}
        args += list(stacked)
    return pl.pallas_call(
        functools.partial(_kv_rows_kernel, heads=heads),
        out_shape=(shape, shape),
        grid=(rows // tr,),
        in_specs=in_specs,
        out_specs=(out_spec, out_spec),
        input_output_aliases=aliases,
        compiler_params=_params("arbitrary"),
        name="kv_rows",
    )(*args)


def _outproj_kernel(ohp_ref, ohs_ref, ofp_ref, ofs_ref, x_ref, wa_ref, wb_ref, npost_ref, npre_ref,
                    x1_ref, h2_ref, *, prompt_tiles):
    is_prompt = pl.program_id(0) < prompt_tiles
    o_hg = jnp.where(is_prompt, ohp_ref[...], ohs_ref[...])
    o_fx = jnp.where(is_prompt, ofp_ref[...], ofs_ref[...].astype(BF16))
    y = (jnp.dot(o_hg, wa_ref[...], preferred_element_type=F32)
         + jnp.dot(o_fx, wb_ref[...], preferred_element_type=F32))
    x1 = x_ref[...] + _rms(y) * npost_ref[...]
    x1_ref[...] = x1
    h2_ref[...] = (_rms(x1) * npre_ref[...]).astype(BF16)


def _outproj(ohg_p, ohg_s, ofx_p, ofx_s, x, w_out, n_post, n_pre, *, layer, tm):
    m, d = x.shape
    wd = ohg_p.shape[1]
    prompt_tiles = ohg_p.shape[0] // tm
    row = lambda i: (i, 0)
    prow = lambda i: (jnp.minimum(i, prompt_tiles - 1), 0)
    srow = lambda i: (jnp.maximum(i - prompt_tiles, 0), 0)
    fixed = lambda i: (0, 0)
    return pl.pallas_call(
        functools.partial(_outproj_kernel, prompt_tiles=prompt_tiles),
        out_shape=(jax.ShapeDtypeStruct((m, d), F32), jax.ShapeDtypeStruct((m, d), BF16)),
        grid=(m // tm,),
        in_specs=[pl.BlockSpec((tm, wd), prow), pl.BlockSpec((tm, wd), srow),
                  pl.BlockSpec((tm, wd), prow), pl.BlockSpec((tm, wd), srow),
                  pl.BlockSpec((tm, d), row),
                  pl.BlockSpec((None, wd, d), lambda i: (layer, 0, 0)),
                  pl.BlockSpec((None, wd, d), lambda i: (layer, 1, 0)),
                  pl.BlockSpec((1, d), fixed), pl.BlockSpec((1, d), fixed)],
        out_specs=(pl.BlockSpec((tm, d), row), pl.BlockSpec((tm, d), row)),
        compiler_params=_params("arbitrary"),
        name="outproj",
    )(ohg_p, ohg_s, ofx_p, ofx_s, x, w_out, w_out, n_post, n_pre)


def _ffn_kernel(h_ref, x_ref, wg_ref, wu_ref, wd_ref, npost_ref, o_ref):
    j = pl.program_id(1)

    @pl.when(j == 0)
    def _():
        o_ref[...] = jnp.zeros_like(o_ref)

    h = h_ref[...]
    a = jnp.dot(h, wg_ref[...], preferred_element_type=F32)
    b = jnp.dot(h, wu_ref[...], preferred_element_type=F32)
    sig_a, _, _ = _sigmoid_parts(a)
    o_ref[...] += jnp.dot((a * sig_a * b).astype(BF16), wd_ref[...], preferred_element_type=F32)

    @pl.when(j == pl.num_programs(1) - 1)
    def _():
        o_ref[...] = x_ref[...] + _rms(o_ref[...]) * npost_ref[...]


def _ffn(h2, x1, w_g, w_u, w_d, n_post, *, layer, tm, tf):
    m, d = x1.shape
    f = w_g.shape[2]
    return pl.pallas_call(
        _ffn_kernel,
        out_shape=jax.ShapeDtypeStruct((m, d), F32),
        grid=(m // tm, f // tf),
        in_specs=[pl.BlockSpec((tm, d), lambda i, j: (i, 0)),
                  pl.BlockSpec((tm, d), lambda i, j: (i, 0)),
                  pl.BlockSpec((None, d, tf), lambda i, j: (layer, 0, j)),
                  pl.BlockSpec((None, d, tf), lambda i, j: (layer, 0, j)),
                  pl.BlockSpec((None, tf, d), lambda i, j: (layer, j, 0)),
                  pl.BlockSpec((1, d), lambda i, j: (0, 0))],
        out_specs=pl.BlockSpec((tm, d), lambda i, j: (i, 0)),
        compiler_params=_params("arbitrary", "arbitrary"),
        name="ffn",
    )(h2, x1, w_g, w_u, w_d, n_post)


def _tiles(batch, seq, nseq, steps, d_ff):
    m = batch * seq + nseq * steps
    tm = next(t for t in (1024, 512, 256, 128) if seq % t == 0 and m % t == 0)
    tm_out = next(t for t in (512, 256, 128) if (batch * seq) % t == 0 and (nseq * steps) % t == 0)
    tm_ffn = next(t for t in (768, 512, 256, 128) if m % t == 0)
    tf = next(t for t in (512, 256, 128) if d_ff % t == 0)
    tt = next(t for t in (512, 256, 128, HGRN_CHUNK) if seq % t == 0)
    tq = next(t for t in (1024, 512, 256, 128) if seq % t == 0)
    bb = next(b for b in (16, 8, 4, 2, 1) if nseq % b == 0 and (batch * seq) % (b * steps) == 0)
    tr = next(t for t in (512, 256, 128) if (batch * seq) % t == 0)
    return dict(tm=tm, tn=1024, tm_out=tm_out, tm_ffn=tm_ffn, tf=tf, tt=tt, tq=tq, bb=bb, tr=tr, hp=4)


def kernel(x_prompt, x_sample, cache_k, cache_v, cache_logf, state_hgrn, page_table, w_in, b_fox_f,
           hg_lower_bounds, hg_norm_w, w_out, norm_mix_pre, norm_mix_post, norm_ffn_pre, norm_ffn_post,
           w_gate, w_up, w_down):
    batch, seq, d = x_prompt.shape
    nseq, steps, _ = x_sample.shape
    depth, n_pool, page, fh, fd = cache_k.shape
    hh, hk, hv = state_hgrn.shape[2:]
    d_ff = w_gate.shape[2]
    assert hk == LANES and hv == LANES and fd == LANES and hh == fh
    heads = hh
    n_main = 7 * heads * LANES
    assert w_in.shape[2] == n_main + fh and fh <= LANES
    mp = batch * seq
    t = _tiles(batch, seq, nseq, steps, d_ff)
    assert seq % 16 == 0

    x = jnp.concatenate([x_prompt.reshape(mp, d), x_sample.reshape(nseq * steps, d)], axis=0)
    row = lambda a: a.reshape(1, -1).astype(F32)
    cache_logf_t = cache_logf.transpose(0, 1, 3, 2)
    w_in_b, w_out_b = w_in.astype(BF16), w_out.astype(BF16)
    w_gate_b, w_up_b, w_down_b = w_gate.astype(BF16), w_up.astype(BF16), w_down.astype(BF16)

    outs = {n: [] for n in ("gp", "sp", "ks", "vs", "gs")}
    hgrn_sample_states = kv_prompt = None
    for l in range(depth):
        w_ff = jnp.pad(w_in[l, :, n_main:], ((0, 0), (0, LANES - fh))).astype(BF16)
        b_ff = jnp.pad(b_fox_f[l].astype(F32), (0, LANES - fh)).reshape(1, LANES)
        z, logf, c = _inproj(x, row(norm_mix_pre[l]), w_in_b, w_ff, b_ff, layer=l, n=n_main, tm=t["tm"],
                             tn=t["tn"], seq_len=seq)

        gn = row(hg_norm_w[l])
        lb_raw = hg_lower_bounds.astype(F32)
        ohg_p, s_p = _hgrn_prompt(z, lb_raw, gn, batch=batch, seq=seq, heads=heads, layer=l,
                                  chunk=HGRN_CHUNK, tt=t["tt"], hp=t["hp"])
        ohg_s, hgrn_sample_states = _hgrn_sample(z, lb_raw, gn, state_hgrn, hgrn_sample_states, row0=mp,
                                                 nseq=nseq, steps=steps, heads=heads, layer=l, bb=t["bb"])

        ofx_p = _fox_prompt(z, c, batch=batch, seq=seq, heads=heads, tq=t["tq"])
        ofx_s = _fox_decode(z, logf, cache_k, cache_v, cache_logf_t, page_table, row0=mp, layer=l, heads=heads,
                            steps=steps)

        hw = heads * LANES
        x1, h2 = _outproj(ohg_p, ohg_s, ofx_p, ofx_s, x, w_out_b, row(norm_mix_post[l]), row(norm_ffn_pre[l]),
                          layer=l, tm=t["tm_out"])
        x = _ffn(h2, x1, w_gate_b, w_up_b, w_down_b, row(norm_ffn_post[l]), layer=l,
                 tm=t["tm_ffn"], tf=t["tf"])

        kcol, vcol = 5 * heads * LANES, 6 * heads * LANES
        kv_prompt = _kv_rows(z, kv_prompt, depth=depth, rows=mp, heads=heads, layer=l, tr=t["tr"])
        outs["gp"].append(logf[:mp, :fh].reshape(batch, seq, fh))
        outs["sp"].append(s_p)
        outs["ks"].append(z[mp:, kcol:kcol + hw].reshape(nseq, steps, fh, fd))
        outs["vs"].append(z[mp:, vcol:vcol + hw].reshape(nseq, steps, fh, fd))
        outs["gs"].append(logf[mp:, :fh].reshape(nseq, steps, fh))

    st = {n: jnp.stack(v) for n, v in outs.items()}
    k_prompt, v_prompt = (a.reshape(depth, batch, seq, fh, fd) for a in kv_prompt)
    return (x[:mp].reshape(batch, seq, d), x[mp:].reshape(nseq, steps, d),
            k_prompt, v_prompt, st["gp"], st["sp"], st["ks"], st["vs"], st["gs"], hgrn_sample_states)
```

```python
import functools

import jax
import jax.numpy as jnp
from jax import lax
from jax.experimental import pallas as pl
from jax.experimental.pallas import tpu as pltpu

NORM_EPS = 1e-6
LOG2E = 1.4426950408889634
LANES = 128
SUBLANES = 8
VMEM_LIMIT_BYTES = 56 * 1024 * 1024
HGRN_CHUNK = 16
HGRN_ATT_ROWS = 256

F32 = jnp.float32
BF16 = jnp.bfloat16

_NT = (((1,), (1,)), ((), ()))
_TN = (((0,), (0,)), ((), ()))


def _params(*sem):
    return pltpu.CompilerParams(dimension_semantics=sem, vmem_limit_bytes=VMEM_LIMIT_BYTES)


def _rms(x):
    return x * lax.rsqrt(jnp.mean(x * x, axis=-1, keepdims=True) + NORM_EPS)


def _sigmoid_parts(z):
    e = jnp.exp(-jnp.abs(z))
    r = 1.0 / (1.0 + e)
    pos = z >= 0
    return jnp.where(pos, r, e * r), jnp.where(pos, e * r, r), jnp.minimum(z, 0.0) - jnp.log1p(e)


def _cumsum_rows(x, period=None):
    n = x.shape[0]
    span = n if period is None else period
    row = lax.broadcasted_iota(jnp.int32, x.shape, 0)
    if period is not None:
        row = row % period
    s = 1
    while s < span:
        x = x + jnp.where(row >= s, pltpu.roll(x, s, axis=0), 0.0)
        s *= 2
    return x


def _cumsum_lanes(x):
    n = x.shape[-1]
    lane = lax.broadcasted_iota(jnp.int32, x.shape, x.ndim - 1)
    s = 1
    while s < n:
        x = x + jnp.where(lane >= s, pltpu.roll(x, s, axis=x.ndim - 1), 0.0)
        s *= 2
    return x


def _inproj_kernel(x_ref, nw_ref, w_ref, wff_ref, bf_ref, z_ref, logf_ref, c_ref,
                   h_scr, carry_scr, *, seq_tiles):
    i = pl.program_id(0)
    j = pl.program_id(1)

    @pl.when(j == 0)
    def _():
        hb = (_rms(x_ref[...]) * nw_ref[...]).astype(BF16)
        h_scr[...] = hb
        ff = jnp.dot(hb, wff_ref[...], preferred_element_type=F32) + bf_ref[...]
        _, _, lf = _sigmoid_parts(ff)
        logf_ref[...] = lf

        @pl.when(i % seq_tiles == 0)
        def _():
            carry_scr[...] = jnp.zeros_like(carry_scr)

        c = _cumsum_rows(lf) + carry_scr[...]
        c_ref[...] = c
        carry_scr[...] = c[c.shape[0] - 1:, :]

    z_ref[...] = jnp.dot(h_scr[...], w_ref[...], preferred_element_type=F32)


def _inproj(x, nw, w_in, w_ff, b_ff, *, layer, n, tm, tn, seq_len):
    m, d = x.shape
    kern = functools.partial(_inproj_kernel, seq_tiles=seq_len // tm)
    return pl.pallas_call(
        kern,
        out_shape=(jax.ShapeDtypeStruct((m, n), F32),
                   jax.ShapeDtypeStruct((m, LANES), F32),
                   jax.ShapeDtypeStruct((m, LANES), F32)),
        grid=(m // tm, n // tn),
        in_specs=[pl.BlockSpec((tm, d), lambda i, j: (i, 0)),
                  pl.BlockSpec((1, d), lambda i, j: (0, 0)),
                  pl.BlockSpec((None, d, tn), lambda i, j: (layer, 0, j)),
                  pl.BlockSpec((d, LANES), lambda i, j: (0, 0)),
                  pl.BlockSpec((1, LANES), lambda i, j: (0, 0))],
        out_specs=(pl.BlockSpec((tm, tn), lambda i, j: (i, j)),
                   pl.BlockSpec((tm, LANES), lambda i, j: (i, 0)),
                   pl.BlockSpec((tm, LANES), lambda i, j: (i, 0))),
        scratch_shapes=[pltpu.VMEM((tm, d), BF16), pltpu.VMEM((1, LANES), F32)],
        compiler_params=_params("arbitrary", "arbitrary"),
        name="inproj",
    )(x, nw, w_in, w_ff, b_ff)


def _lower_bound(raw, layer):
    depth = raw.shape[0]
    rows = [raw[i:i + 1, :] for i in range(depth)]
    mx = functools.reduce(jnp.maximum, rows)
    ex = [jnp.exp(r - mx) for r in rows]
    tot = functools.reduce(lambda a, b: a + b, ex)
    p = [e / tot for e in ex]
    cum = functools.reduce(lambda a, b: a + b, p[:layer + 1])
    return cum - p[0]


def _hgrn_gates(lb, qr, z, chunk):
    sig_q, _, _ = _sigmoid_parts(qr)
    q = qr * sig_q
    _, sig_nz, logsig = _sigmoid_parts(z)
    a = jnp.log(lb)
    bb = jnp.log1p(-lb) + logsig
    logf = jnp.maximum(a, bb) + jnp.log1p(jnp.exp(-jnp.abs(a - bb)))
    k = (1.0 - lb) * sig_nz
    return q, k, _cumsum_rows(logf, period=chunk)


def _hgrn_tile(q, k, v, b, chunk, pair=False):
    rows = q.shape[0]
    chunks = [slice(c * chunk, (c + 1) * chunk) for c in range(rows // chunk)]
    spread = lambda r: jnp.broadcast_to(r, (chunk, r.shape[1]))
    ref = jnp.concatenate([spread(b[s.start + chunk // 2 - 1:s.start + chunk // 2]) for s in chunks], axis=0)
    ends = [b[s.stop - 1:s.stop] for s in chunks]
    end = jnp.concatenate([spread(e) for e in ends], axis=0)
    vb = v.astype(BF16)
    qi = (q * jnp.exp(b - ref)).astype(BF16)
    ki = (k * jnp.exp(ref - b)).astype(BF16)
    g = min(rows, HGRN_ATT_ROWS)
    r = lax.broadcasted_iota(jnp.int32, (g, g), 0)
    c = lax.broadcasted_iota(jnp.int32, (g, g), 1)
    keep = (jnp.bitwise_and(r, -chunk) == jnp.bitwise_and(c, -chunk)) & (r >= c)
    qs = q * jnp.exp(b)
    ks = k * jnp.exp(end - b)
    if pair:
        second_on_first = ((jnp.bitwise_and(r, -2 * chunk) == jnp.bitwise_and(c, -2 * chunk))
                           & (jnp.bitwise_and(r, chunk) != 0) & (jnp.bitwise_and(c, chunk) == 0))
        qs_b, ks_b = qs.astype(BF16), ks.astype(BF16)
    parts = []
    for s in (slice(i, i + g) for i in range(0, rows, g)):
        att = jnp.where(keep, lax.dot_general(qi[s], ki[s], _NT, preferred_element_type=F32), 0.0)
        if pair:
            cross = lax.dot_general(qs_b[s], ks_b[s], _NT, preferred_element_type=F32)
            att = jnp.where(second_on_first, cross, att)
        parts.append(jnp.dot(att.astype(BF16), vb[s], preferred_element_type=F32))
    intra = jnp.concatenate(parts, axis=0)
    incs = [lax.dot_general(v[s].astype(BF16), ks[s].astype(BF16), _TN, preferred_element_type=F32)
            for s in chunks]
    if not pair:
        return [(s, intra[s], qs[s].astype(BF16), jnp.exp(e), u) for s, e, u in zip(chunks, ends, incs)]
    steps = []
    for c0 in range(0, len(chunks), 2):
        s0, s1 = chunks[c0], chunks[c0 + 1]
        d0, d1 = jnp.exp(ends[c0]), jnp.exp(ends[c0 + 1])
        both = slice(s0.start, s1.stop)
        reads = jnp.concatenate([qs[s0], qs[s1] * d0], axis=0).astype(BF16)
        steps.append((both, intra[both], reads, d0 * d1, incs[c0] * d1 + incs[c0 + 1]))
    return steps


def _hgrn_finish(o, g, gn):
    sig_g, _, _ = _sigmoid_parts(g)
    return (_rms(o) * gn * (g * sig_g)).astype(BF16)


def _hgrn_prompt_kernel(lb_ref, q_ref, f_ref, i_ref, g_ref, gn_ref, o_ref, sfin_ref,
                        st_scr, o_scr, *, chunk, layer):
    t = pl.program_id(2)

    @pl.when(t == 0)
    def _():
        st_scr[...] = jnp.zeros_like(st_scr)

    finals = []
    for j in range(st_scr.shape[0]):
        cols = slice(j * LANES, (j + 1) * LANES)
        lb = _lower_bound(lb_ref[:, cols], layer)
        q, k, b = _hgrn_gates(lb, q_ref[:, cols], f_ref[:, cols], chunk)
        st = st_scr[j]
        for sl, intra, qs, decay, inc in _hgrn_tile(q, k, i_ref[:, cols], b, chunk, pair=True):
            o_scr[sl, cols] = intra + lax.dot_general(qs, st.astype(BF16), _NT, preferred_element_type=F32)
            st = st * decay + inc
        st_scr[j] = st
        finals.append(st)
        o_ref[:, cols] = _hgrn_finish(o_scr[:, cols], g_ref[:, cols], gn_ref[...])

    @pl.when(t == pl.num_programs(2) - 1)
    def _():
        for j, st in enumerate(finals):
            sfin_ref[j] = st.T


def _hgrn_prompt(z, lb_raw, gn, *, batch, seq, heads, layer, chunk, tt, hp):
    nt = seq // tt
    groups = heads // hp
    w = hp * LANES
    col = lambda part: (lambda b, h, t: (b * nt + t, part * groups + h))
    kern = functools.partial(_hgrn_prompt_kernel, chunk=chunk, layer=layer)
    depth = lb_raw.shape[0]
    return pl.pallas_call(
        kern,
        out_shape=(jax.ShapeDtypeStruct((batch * seq, heads * LANES), BF16),
                   jax.ShapeDtypeStruct((batch, heads, LANES, LANES), F32)),
        grid=(batch, groups, nt),
        in_specs=[pl.BlockSpec((depth, w), lambda b, h, t: (0, h)),
                  pl.BlockSpec((tt, w), col(0)),
                  pl.BlockSpec((tt, w), col(1)),
                  pl.BlockSpec((tt, w), col(2)),
                  pl.BlockSpec((tt, w), col(3)),
                  pl.BlockSpec((1, LANES), lambda b, h, t: (0, 0))],
        out_specs=(pl.BlockSpec((tt, w), lambda b, h, t: (b * nt + t, h)),
                   pl.BlockSpec((None, hp, LANES, LANES), lambda b, h, t: (b, h, 0, 0))),
        scratch_shapes=[pltpu.VMEM((hp, LANES, LANES), F32), pltpu.VMEM((tt, w), F32)],
        compiler_params=_params("arbitrary", "arbitrary", "arbitrary"),
        name="hgrn_prompt",
    )(lb_raw, z, z, z, z, gn)


def _hgrn_sample_kernel(lb_ref, q_ref, f_ref, i_ref, g_ref, gn_ref, s0_ref, *rest, chunk, layer):
    o_ref, sfin_ref, o_scr = rest[-3:]
    lb = _lower_bound(lb_ref[...], layer)
    q, k, b = _hgrn_gates(lb, q_ref[...], f_ref[...], chunk)
    for s, (sl, intra, qs, decay, inc) in enumerate(_hgrn_tile(q, k, i_ref[...], b, chunk)):
        st = s0_ref[s].T
        o_scr[sl, :] = intra + lax.dot_general(qs, st.astype(BF16), _NT, preferred_element_type=F32)
        sfin_ref[s] = (st * decay + inc).T
    o_ref[...] = _hgrn_finish(o_scr[...], g_ref[...], gn_ref[...])


def _hgrn_sample(z, lb_raw, gn, state, stacked, *, row0, nseq, steps, heads, layer, bb):
    rows = bb * steps
    rb0 = row0 // rows
    col = lambda base: (lambda i, h: (rb0 + i, base + h))
    kern = functools.partial(_hgrn_sample_kernel, chunk=steps, layer=layer)
    depth = lb_raw.shape[0]
    state_spec = pl.BlockSpec((None, bb, None, LANES, LANES), lambda i, h: (layer, i, h, 0, 0))
    in_specs = [pl.BlockSpec((depth, LANES), lambda i, h: (0, h)),
                pl.BlockSpec((rows, LANES), col(0)),
                pl.BlockSpec((rows, LANES), col(heads)),
                pl.BlockSpec((rows, LANES), col(2 * heads)),
                pl.BlockSpec((rows, LANES), col(3 * heads)),
                pl.BlockSpec((1, LANES), lambda i, h: (0, 0)),
                state_spec]
    args = [lb_raw, z, z, z, z, gn, state]
    aliases = {}
    if stacked is not None:
        in_specs.append(pl.BlockSpec(memory_space=pl.ANY))
        aliases = {len(args): 1}
        args.append(stacked)
    return pl.pallas_call(
        kern,
        out_shape=(jax.ShapeDtypeStruct((nseq * steps, heads * LANES), BF16),
                   jax.ShapeDtypeStruct(state.shape, F32)),
        grid=(nseq // bb, heads),
        in_specs=in_specs,
        out_specs=(pl.BlockSpec((rows, LANES), lambda i, h: (i, h)), state_spec),
        scratch_shapes=[pltpu.VMEM((rows, LANES), F32)],
        input_output_aliases=aliases,
        compiler_params=_params("arbitrary", "arbitrary"),
        name="hgrn_sample",
    )(*args)


def _fox_prompt_kernel(q_ref, k_ref, v_ref, c_ref, o_ref, vt_scr, *, tq, scale):
    h = pl.program_id(1)
    qi = pl.program_id(2)

    @pl.when(qi == 0)
    def _():
        vt_scr[...] = v_ref[...].T.astype(BF16)

    q_t = (q_ref[...] * (scale * LOG2E)).T.astype(BF16)
    lane = lax.broadcasted_iota(jnp.int32, (tq, LANES), 1)

    def rows(j):
        return pl.ds(pl.multiple_of(j * tq, tq), tq)

    def scores(j):
        kj = k_ref[rows(j), :].astype(BF16)
        c_keys = jnp.sum(jnp.where(lane == h, c_ref[rows(j), :], 0.0), axis=-1, keepdims=True)
        return jnp.dot(kj, q_t, preferred_element_type=F32) - c_keys * LOG2E

    def update(j, s, m, l, acc):
        m_new = jnp.maximum(m, jnp.max(s, axis=0, keepdims=True))
        alpha = jnp.exp2(m - m_new)
        p = jnp.exp2(s - m_new)
        l = alpha * l + jnp.sum(p, axis=0, keepdims=True)
        acc = alpha * acc + jnp.dot(vt_scr[:, rows(j)], p.astype(BF16), preferred_element_type=F32)
        return m_new, l, acc

    init = (jnp.full((1, tq), -jnp.inf, F32), jnp.zeros((1, tq), F32), jnp.zeros((LANES, tq), F32))
    stats = lax.fori_loop(0, qi, lambda j, st: update(j, scores(j), *st), init)
    causal = (lax.broadcasted_iota(jnp.int32, (tq, tq), 0) <= lax.broadcasted_iota(jnp.int32, (tq, tq), 1))
    _, l, acc = update(qi, jnp.where(causal, scores(qi), -jnp.inf), *stats)
    o_ref[...] = (acc / l).T.astype(BF16)


def _fox_prompt(z, c, *, batch, seq, heads, tq):
    nq = seq // tq
    d = LANES
    qcol, kcol, vcol = 4 * heads, 5 * heads, 6 * heads
    kern = functools.partial(_fox_prompt_kernel, tq=tq, scale=d ** -0.5)
    return pl.pallas_call(
        kern,
        out_shape=jax.ShapeDtypeStruct((batch * seq, heads * d), BF16),
        grid=(batch, heads, nq),
        in_specs=[pl.BlockSpec((tq, d), lambda b, h, i: (b * nq + i, qcol + h)),
                  pl.BlockSpec((seq, d), lambda b, h, i: (b, kcol + h)),
                  pl.BlockSpec((seq, d), lambda b, h, i: (b, vcol + h)),
                  pl.BlockSpec((seq, LANES), lambda b, h, i: (b, 0))],
        out_specs=pl.BlockSpec((tq, d), lambda b, h, i: (b * nq + i, h)),
        scratch_shapes=[pltpu.VMEM((d, seq), BF16)],
        compiler_params=_params("arbitrary", "arbitrary", "arbitrary"),
        name="fox_prompt",
    )(z, z, z, c)


def _fox_decode_kernel(pt_ref, q_ref, kn_ref, vn_ref, lfn_ref, *rest, n_pages, heads, steps, page, scale):
    k_refs, v_refs = rest[:n_pages], rest[n_pages:2 * n_pages]
    lf_ref, o_ref = rest[2 * n_pages:]
    d = LANES
    q = q_ref[...] * scale

    seq_id = pl.program_id(0)
    lf_pages = [lf_ref[pt_ref[seq_id, i]] for i in range(n_pages)]
    pad = lambda a: jnp.concatenate([a, jnp.zeros((page - steps, a.shape[1]), a.dtype)], axis=0)
    lf_pages.append(pad(lfn_ref[...]).T[0:heads, :])
    c_pages, before = [], jnp.zeros((heads, 1), F32)
    for lf in lf_pages:
        c_pages.append(_cumsum_lanes(lf) + before)
        before = before + jnp.sum(lf, axis=-1, keepdims=True)

    qpos = lax.broadcasted_iota(jnp.int32, (steps, page), 0)
    kpos = lax.broadcasted_iota(jnp.int32, (steps, page), 1)
    causal = kpos <= qpos

    def scores(h):
        cols = slice(h * d, (h + 1) * d)
        qh = q[:, cols].astype(BF16)
        keys = [r[pl.ds(h, page, stride=heads), :] for r in k_refs] + [pad(kn_ref[:, cols])]
        s = [lax.dot_general(qh, k.astype(BF16), _NT, preferred_element_type=F32) - c[h:h + 1, :]
             for k, c in zip(keys, c_pages)]
        s[-1] = jnp.where(causal, s[-1], -jnp.inf)
        return s

    def attend(h, s):
        cols = slice(h * d, (h + 1) * d)
        vals = [r[pl.ds(h, page, stride=heads), :] for r in v_refs] + [pad(vn_ref[:, cols])]
        m = functools.reduce(jnp.maximum, s)
        m = jnp.max(m, axis=-1, keepdims=True)
        p = [jnp.exp(t - m) for t in s]
        l = jnp.sum(functools.reduce(lambda a, b: a + b, p), axis=-1, keepdims=True)
        acc = functools.reduce(lambda a, b: a + b,
                               [jnp.dot(t.astype(BF16), v.astype(BF16), preferred_element_type=F32)
                                for t, v in zip(p, vals)])
        o_ref[:, cols] = acc / l

    ahead = min(2, heads)
    pending = [scores(h) for h in range(ahead)]
    for h in range(heads):
        if h + ahead < heads:
            pending.append(scores(h + ahead))
        attend(h, pending.pop(0))


def _fox_decode(z, logf, cache_k, cache_v, cache_logf_t, page_table, *, row0, layer, heads, steps):
    nseq, n_pages = page_table.shape
    depth, n_pool, page = cache_k.shape[:3]
    assert page == LANES
    d = LANES
    w = heads * d
    rb0 = row0 // steps
    qcol, kcol, vcol = 4, 5, 6
    ck = cache_k.reshape(depth, n_pool, page * heads, d)
    cv = cache_v.reshape(depth, n_pool, page * heads, d)
    kern = functools.partial(_fox_decode_kernel, n_pages=n_pages, heads=heads, steps=steps, page=page,
                             scale=d ** -0.5)
    pg = lambda i: (lambda b, pt: (layer, pt[b, i], 0, 0))
    kv_spec = [pl.BlockSpec((None, None, page * heads, d), pg(i)) for i in range(n_pages)]
    lf_spec = pl.BlockSpec((None, n_pool, heads, page), lambda b, pt: (layer, 0, 0, 0),
                           pipeline_mode=pl.Buffered(1))
    grid_spec = pltpu.PrefetchScalarGridSpec(
        num_scalar_prefetch=1,
        grid=(nseq,),
        in_specs=[pl.BlockSpec((steps, w), lambda b, pt: (rb0 + b, qcol)),
                  pl.BlockSpec((steps, w), lambda b, pt: (rb0 + b, kcol)),
                  pl.BlockSpec((steps, w), lambda b, pt: (rb0 + b, vcol)),
                  pl.BlockSpec((steps, LANES), lambda b, pt: (rb0 + b, 0))]
                 + kv_spec + kv_spec + [lf_spec],
        out_specs=pl.BlockSpec((steps, w), lambda b, pt: (b, 0)),
    )
    return pl.pallas_call(
        kern,
        out_shape=jax.ShapeDtypeStruct((nseq * steps, w), F32),
        grid_spec=grid_spec,
        compiler_params=_params("arbitrary"),
        name="fox_decode",
    )(page_table, z, z, z, logf, *([ck] * n_pages), *([cv] * n_pages), cache_logf_t)


def _kv_rows_kernel(k_ref, v_ref, *rest, heads):
    ko_ref, vo_ref = rest[-2:]
    rows = k_ref.shape[0]
    for src, dst in ((k_ref, ko_ref), (v_ref, vo_ref)):
        for h in range(heads):
            dst[pl.ds(h, rows, stride=heads), :] = src[:, h * LANES:(h + 1) * LANES]


def _kv_rows(z, stacked, *, depth, rows, heads, layer, tr):
    w = heads * LANES
    kcol, vcol = 5, 6
    shape = jax.ShapeDtypeStruct((depth, rows * heads, LANES), F32)
    out_spec = pl.BlockSpec((None, tr * heads, LANES), lambda i: (layer, i, 0))
    in_specs = [pl.BlockSpec((tr, w), lambda i: (i, kcol)), pl.BlockSpec((tr, w), lambda i: (i, vcol))]
    args = [z, z]
    aliases = {}
    if stacked is not None:
        in_specs += [pl.BlockSpec(memory_space=pl.ANY)] * 2
        aliases = {2: 0, 3: 1}
        args += list(stacked)
    return pl.pallas_call(
        functools.partial(_kv_rows_kernel, heads=heads),
        out_shape=(shape, shape),
        grid=(rows // tr,),
        in_specs=in_specs,
        out_specs=(out_spec, out_spec),
        input_output_aliases=aliases,
        compiler_params=_params("arbitrary"),
        name="kv_rows",
    )(*args)


def _outproj_kernel(ohp_ref, ohs_ref, ofp_ref, ofs_ref, x_ref, wa_ref, wb_ref, npost_ref, npre_ref,
                    x1_ref, h2_ref, *, prompt_tiles):
    is_prompt = pl.program_id(0) < prompt_tiles
    o_hg = jnp.where(is_prompt, ohp_ref[...], ohs_ref[...])
    o_fx = jnp.where(is_prompt, ofp_ref[...], ofs_ref[...].astype(BF16))
    y = (jnp.dot(o_hg, wa_ref[...], preferred_element_type=F32)
         + jnp.dot(o_fx, wb_ref[...], preferred_element_type=F32))
    x1 = x_ref[...] + _rms(y) * npost_ref[...]
    x1_ref[...] = x1
    h2_ref[...] = (_rms(x1) * npre_ref[...]).astype(BF16)


def _outproj(ohg_p, ohg_s, ofx_p, ofx_s, x, w_out, n_post, n_pre, *, layer, tm):
    m, d = x.shape
    wd = ohg_p.shape[1]
    prompt_tiles = ohg_p.shape[0] // tm
    row = lambda i: (i, 0)
    prow = lambda i: (jnp.minimum(i, prompt_tiles - 1), 0)
    srow = lambda i: (jnp.maximum(i - prompt_tiles, 0), 0)
    fixed = lambda i: (0, 0)
    return pl.pallas_call(
        functools.partial(_outproj_kernel, prompt_tiles=prompt_tiles),
        out_shape=(jax.ShapeDtypeStruct((m, d), F32), jax.ShapeDtypeStruct((m, d), BF16)),
        grid=(m // tm,),
        in_specs=[pl.BlockSpec((tm, wd), prow), pl.BlockSpec((tm, wd), srow),
                  pl.BlockSpec((tm, wd), prow), pl.BlockSpec((tm, wd), srow),
                  pl.BlockSpec((tm, d), row),
                  pl.BlockSpec((None, wd, d), lambda i: (layer, 0, 0)),
                  pl.BlockSpec((None, wd, d), lambda i: (layer, 1, 0)),
                  pl.BlockSpec((1, d), fixed), pl.BlockSpec((1, d), fixed)],
        out_specs=(pl.BlockSpec((tm, d), row), pl.BlockSpec((tm, d), row)),
        compiler_params=_params("arbitrary"),
        name="outproj",
    )(ohg_p, ohg_s, ofx_p, ofx_s, x, w_out, w_out, n_post, n_pre)


def _ffn_kernel(h_ref, x_ref, wg_ref, wu_ref, wd_ref, npost_ref, o_ref):
    j = pl.program_id(1)

    @pl.when(j == 0)
    def _():
        o_ref[...] = jnp.zeros_like(o_ref)

    h = h_ref[...]
    a = jnp.dot(h, wg_ref[...], preferred_element_type=F32)
    b = jnp.dot(h, wu_ref[...], preferred_element_type=F32)
    sig_a, _, _ = _sigmoid_parts(a)
    o_ref[...] += jnp.dot((a * sig_a * b).astype(BF16), wd_ref[...], preferred_element_type=F32)

    @pl.when(j == pl.num_programs(1) - 1)
    def _():
        o_ref[...] = x_ref[...] + _rms(o_ref[...]) * npost_ref[...]


def _ffn(h2, x1, w_g, w_u, w_d, n_post, *, layer, tm, tf):
    m, d = x1.shape
    f = w_g.shape[2]
    return pl.pallas_call(
        _ffn_kernel,
        out_shape=jax.ShapeDtypeStruct((m, d), F32),
        grid=(m // tm, f // tf),
        in_specs=[pl.BlockSpec((tm, d), lambda i, j: (i, 0)),
                  pl.BlockSpec((tm, d), lambda i, j: (i, 0)),
                  pl.BlockSpec((None, d, tf), lambda i, j: (layer, 0, j)),
                  pl.BlockSpec((None, d, tf), lambda i, j: (layer, 0, j)),
                  pl.BlockSpec((None, tf, d), lambda i, j: (layer, j, 0)),
                  pl.BlockSpec((1, d), lambda i, j: (0, 0))],
        out_specs=pl.BlockSpec((tm, d), lambda i, j: (i, 0)),
        compiler_params=_params("arbitrary", "arbitrary"),
        name="ffn",
    )(h2, x1, w_g, w_u, w_d, n_post)


def _tiles(batch, seq, nseq, steps, d_ff):
    m = batch * seq + nseq * steps
    tm = next(t for t in (1024, 512, 256, 128) if seq % t == 0 and m % t == 0)
    tm_out = next(t for t in (512, 256, 128) if (batch * seq) % t == 0 and (nseq * steps) % t == 0)
    tm_ffn = next(t for t in (768, 512, 256, 128) if m % t == 0)
    tf = next(t for t in (512, 256, 128) if d_ff % t == 0)
    tt = next(t for t in (512, 256, 128, HGRN_CHUNK) if seq % t == 0)
    tq = next(t for t in (1024, 512, 256, 128) if seq % t == 0)
    bb = next(b for b in (16, 8, 4, 2, 1) if nseq % b == 0 and (batch * seq) % (b * steps) == 0)
    tr = next(t for t in (512, 256, 128) if (batch * seq) % t == 0)
    return dict(tm=tm, tn=1024, tm_out=tm_out, tm_ffn=tm_ffn, tf=tf, tt=tt, tq=tq, bb=bb, tr=tr, hp=4)


def kernel(x_prompt, x_sample, cache_k, cache_v, cache_logf, state_hgrn, page_table, w_in, b_fox_f,
           hg_lower_bounds, hg_norm_w, w_out, norm_mix_pre, norm_mix_post, norm_ffn_pre, norm_ffn_post,
           w_gate, w_up, w_down):
    batch, seq, d = x_prompt.shape
    nseq, steps, _ = x_sample.shape
    depth, n_pool, page, fh, fd = cache_k.shape
    hh, hk, hv = state_hgrn.shape[2:]
    d_ff = w_gate.shape[2]
    assert hk == LANES and hv == LANES and fd == LANES and hh == fh
    heads = hh
    n_main = 7 * heads * LANES
    assert w_in.shape[2] == n_main + fh and fh <= LANES
    mp = batch * seq
    t = _tiles(batch, seq, nseq, steps, d_ff)
    assert seq % 16 == 0

    x = jnp.concatenate([x_prompt.reshape(mp, d), x_sample.reshape(nseq * steps, d)], axis=0)
    row = lambda a: a.reshape(1, -1).astype(F32)
    cache_logf_t = cache_logf.transpose(0, 1, 3, 2)
    w_in_b, w_out_b = w_in.astype(BF16), w_out.astype(BF16)
    w_gate_b, w_up_b, w_down_b = w_gate.astype(BF16), w_up.astype(BF16), w_down.astype(BF16)

    outs = {n: [] for n in ("gp", "sp", "ks", "vs", "gs")}
    hgrn_sample_states = kv_prompt = None
    for l in range(depth):
        w_ff = jnp.pad(w_in[l, :, n_main:], ((0, 0), (0, LANES - fh))).astype(BF16)
        b_ff = jnp.pad(b_fox_f[l].astype(F32), (0, LANES - fh)).reshape(1, LANES)
        z, logf, c = _inproj(x, row(norm_mix_pre[l]), w_in_b, w_ff, b_ff, layer=l, n=n_main, tm=t["tm"],
                             tn=t["tn"], seq_len=seq)

        gn = row(hg_norm_w[l])
        lb_raw = hg_lower_bounds.astype(F32)
        ohg_p, s_p = _hgrn_prompt(z, lb_raw, gn, batch=batch, seq=seq, heads=heads, layer=l,
                                  chunk=HGRN_CHUNK, tt=t["tt"], hp=t["hp"])
        ohg_s, hgrn_sample_states = _hgrn_sample(z, lb_raw, gn, state_hgrn, hgrn_sample_states, row0=mp,
                                                 nseq=nseq, steps=steps, heads=heads, layer=l, bb=t["bb"])

        ofx_p = _fox_prompt(z, c, batch=batch, seq=seq, heads=heads, tq=t["tq"])
        ofx_s = _fox_decode(z, logf, cache_k, cache_v, cache_logf_t, page_table, row0=mp, layer=l, heads=heads,
                            steps=steps)

        hw = heads * LANES
        x1, h2 = _outproj(ohg_p, ohg_s, ofx_p, ofx_s, x, w_out_b, row(norm_mix_post[l]), row(norm_ffn_pre[l]),
                          layer=l, tm=t["tm_out"])
        x = _ffn(h2, x1, w_gate_b, w_up_b, w_down_b, row(norm_ffn_post[l]), layer=l,
                 tm=t["tm_ffn"], tf=t["tf"])

        kcol, vcol = 5 * heads * LANES, 6 * heads * LANES
        kv_prompt = _kv_rows(z, kv_prompt, depth=depth, rows=mp, heads=heads, layer=l, tr=t["tr"])
        outs["gp"].append(logf[:mp, :fh].reshape(batch, seq, fh))
        outs["sp"].append(s_p)
        outs["ks"].append(z[mp:, kcol:kcol + hw].reshape(nseq, steps, fh, fd))
        outs["vs"].append(z[mp:, vcol:vcol + hw].reshape(nseq, steps, fh, fd))
        outs["gs"].append(logf[mp:, :fh].reshape(nseq, steps, fh))

    st = {n: jnp.stack(v) for n, v in outs.items()}
    k_prompt, v_prompt = (a.reshape(depth, batch, seq, fh, fd) for a in kv_prompt)
    return (x[:mp].reshape(batch, seq, d), x[mp:].reshape(nseq, steps, d),
            k_prompt, v_prompt, st["gp"], st["sp"], st["ks"], st["vs"], st["gs"], hgrn_sample_states)
```

```python
import functools

import jax
import jax.numpy as jnp
from jax import lax
from jax.experimental import pallas as pl
from jax.experimental.pallas import tpu as pltpu

NORM_EPS = 1e-6
LOG2E = 1.4426950408889634
LANES = 128
SUBLANES = 8
VMEM_LIMIT_BYTES = 56 * 1024 * 1024
HGRN_CHUNK = 16
HGRN_ATT_ROWS = 256

F32 = jnp.float32
BF16 = jnp.bfloat16

_NT = (((1,), (1,)), ((), ()))
_TN = (((0,), (0,)), ((), ()))


def _params(*sem):
    return pltpu.CompilerParams(dimension_semantics=sem, vmem_limit_bytes=VMEM_LIMIT_BYTES)


def _rms(x):
    return x * lax.rsqrt(jnp.mean(x * x, axis=-1, keepdims=True) + NORM_EPS)


def _sigmoid_parts(z):
    e = jnp.exp(-jnp.abs(z))
    r = 1.0 / (1.0 + e)
    pos = z >= 0
    return jnp.where(pos, r, e * r), jnp.where(pos, e * r, r), jnp.minimum(z, 0.0) - jnp.log1p(e)


def _cumsum_rows(x, period=None):
    n = x.shape[0]
    span = n if period is None else period
    row = lax.broadcasted_iota(jnp.int32, x.shape, 0)
    if period is not None:
        row = row % period
    s = 1
    while s < span:
        x = x + jnp.where(row >= s, pltpu.roll(x, s, axis=0), 0.0)
        s *= 2
    return x


def _cumsum_lanes(x):
    n = x.shape[-1]
    lane = lax.broadcasted_iota(jnp.int32, x.shape, x.ndim - 1)
    s = 1
    while s < n:
        x = x + jnp.where(lane >= s, pltpu.roll(x, s, axis=x.ndim - 1), 0.0)
        s *= 2
    return x


def _inproj_kernel(x_ref, nw_ref, w_ref, wff_ref, bf_ref, z_ref, logf_ref, c_ref,
                   h_scr, carry_scr, *, seq_tiles):
    i = pl.program_id(0)
    j = pl.program_id(1)

    @pl.when(j == 0)
    def _():
        hb = (_rms(x_ref[...]) * nw_ref[...]).astype(BF16)
        h_scr[...] = hb
        ff = jnp.dot(hb, wff_ref[...], preferred_element_type=F32) + bf_ref[...]
        _, _, lf = _sigmoid_parts(ff)
        logf_ref[...] = lf

        @pl.when(i % seq_tiles == 0)
        def _():
            carry_scr[...] = jnp.zeros_like(carry_scr)

        c = _cumsum_rows(lf) + carry_scr[...]
        c_ref[...] = c
        carry_scr[...] = c[c.shape[0] - 1:, :]

    z_ref[...] = jnp.dot(h_scr[...], w_ref[...], preferred_element_type=F32)


def _inproj(x, nw, w_in, w_ff, b_ff, *, layer, n, tm, tn, seq_len):
    m, d = x.shape
    kern = functools.partial(_inproj_kernel, seq_tiles=seq_len // tm)
    return pl.pallas_call(
        kern,
        out_shape=(jax.ShapeDtypeStruct((m, n), F32),
                   jax.ShapeDtypeStruct((m, LANES), F32),
                   jax.ShapeDtypeStruct((m, LANES), F32)),
        grid=(m // tm, n // tn),
        in_specs=[pl.BlockSpec((tm, d), lambda i, j: (i, 0)),
                  pl.BlockSpec((1, d), lambda i, j: (0, 0)),
                  pl.BlockSpec((None, d, tn), lambda i, j: (layer, 0, j)),
                  pl.BlockSpec((d, LANES), lambda i, j: (0, 0)),
                  pl.BlockSpec((1, LANES), lambda i, j: (0, 0))],
        out_specs=(pl.BlockSpec((tm, tn), lambda i, j: (i, j)),
                   pl.BlockSpec((tm, LANES), lambda i, j: (i, 0)),
                   pl.BlockSpec((tm, LANES), lambda i, j: (i, 0))),
        scratch_shapes=[pltpu.VMEM((tm, d), BF16), pltpu.VMEM((1, LANES), F32)],
        compiler_params=_params("arbitrary", "arbitrary"),
        name="inproj",
    )(x, nw, w_in, w_ff, b_ff)


def _lower_bound(raw, layer):
    depth = raw.shape[0]
    rows = [raw[i:i + 1, :] for i in range(depth)]
    mx = functools.reduce(jnp.maximum, rows)
    ex = [jnp.exp(r - mx) for r in rows]
    tot = functools.reduce(lambda a, b: a + b, ex)
    p = [e / tot for e in ex]
    cum = functools.reduce(lambda a, b: a + b, p[:layer + 1])
    return cum - p[0]


def _hgrn_gates(lb, qr, z, chunk):
    sig_q, _, _ = _sigmoid_parts(qr)
    q = qr * sig_q
    _, sig_nz, logsig = _sigmoid_parts(z)
    a = jnp.log(lb)
    bb = jnp.log1p(-lb) + logsig
    logf = jnp.maximum(a, bb) + jnp.log1p(jnp.exp(-jnp.abs(a - bb)))
    k = (1.0 - lb) * sig_nz
    return q, k, _cumsum_rows(logf, period=chunk)


def _hgrn_tile(q, k, v, b, chunk, pair=False):
    rows = q.shape[0]
    chunks = [slice(c * chunk, (c + 1) * chunk) for c in range(rows // chunk)]
    spread = lambda r: jnp.broadcast_to(r, (chunk, r.shape[1]))
    ref = jnp.concatenate([spread(b[s.start + chunk // 2 - 1:s.start + chunk // 2]) for s in chunks], axis=0)
    ends = [b[s.stop - 1:s.stop] for s in chunks]
    end = jnp.concatenate([spread(e) for e in ends], axis=0)
    vb = v.astype(BF16)
    qi = (q * jnp.exp(b - ref)).astype(BF16)
    ki = (k * jnp.exp(ref - b)).astype(BF16)
    g = min(rows, HGRN_ATT_ROWS)
    r = lax.broadcasted_iota(jnp.int32, (g, g), 0)
    c = lax.broadcasted_iota(jnp.int32, (g, g), 1)
    keep = (jnp.bitwise_and(r, -chunk) == jnp.bitwise_and(c, -chunk)) & (r >= c)
    qs = q * jnp.exp(b)
    ks = k * jnp.exp(end - b)
    if pair:
        second_on_first = ((jnp.bitwise_and(r, -2 * chunk) == jnp.bitwise_and(c, -2 * chunk))
                           & (jnp.bitwise_and(r, chunk) != 0) & (jnp.bitwise_and(c, chunk) == 0))
        qs_b, ks_b = qs.astype(BF16), ks.astype(BF16)
    parts = []
    for s in (slice(i, i + g) for i in range(0, rows, g)):
        att = jnp.where(keep, lax.dot_general(qi[s], ki[s], _NT, preferred_element_type=F32), 0.0)
        if pair:
            cross = lax.dot_general(qs_b[s], ks_b[s], _NT, preferred_element_type=F32)
            att = jnp.where(second_on_first, cross, att)
        parts.append(jnp.dot(att.astype(BF16), vb[s], preferred_element_type=F32))
    intra = jnp.concatenate(parts, axis=0)
    incs = [lax.dot_general(v[s].astype(BF16), ks[s].astype(BF16), _TN, preferred_element_type=F32)
            for s in chunks]
    if not pair:
        return [(s, intra[s], qs[s].astype(BF16), jnp.exp(e), u) for s, e, u in zip(chunks, ends, incs)]
    steps = []
    for c0 in range(0, len(chunks), 2):
        s0, s1 = chunks[c0], chunks[c0 + 1]
        d0, d1 = jnp.exp(ends[c0]), jnp.exp(ends[c0 + 1])
        both = slice(s0.start, s1.stop)
        reads = jnp.concatenate([qs[s0], qs[s1] * d0], axis=0).astype(BF16)
        steps.append((both, intra[both], reads, d0 * d1, incs[c0] * d1 + incs[c0 + 1]))
    return steps


def _hgrn_finish(o, g, gn):
    sig_g, _, _ = _sigmoid_parts(g)
    return (_rms(o) * gn * (g * sig_g)).astype(BF16)


def _hgrn_prompt_kernel(lb_ref, q_ref, f_ref, i_ref, g_ref, gn_ref, kz_ref, vz_ref, *rest,
                        chunk, layer, heads):
    o_ref, sfin_ref, ko_ref, vo_ref, st_scr, o_scr = rest[-6:]
    t = pl.program_id(2)

    @pl.when(t == 0)
    def _():
        st_scr[...] = jnp.zeros_like(st_scr)

    kv_rows = kz_ref.shape[0]
    for src, dst in ((kz_ref, ko_ref), (vz_ref, vo_ref)):
        for h in range(heads):
            dst[pl.ds(h, kv_rows, stride=heads), :] = src[:, h * LANES:(h + 1) * LANES]

    finals = []
    for j in range(st_scr.shape[0]):
        cols = slice(j * LANES, (j + 1) * LANES)
        lb = _lower_bound(lb_ref[:, cols], layer)
        q, k, b = _hgrn_gates(lb, q_ref[:, cols], f_ref[:, cols], chunk)
        st = st_scr[j]
        for sl, intra, qs, decay, inc in _hgrn_tile(q, k, i_ref[:, cols], b, chunk, pair=True):
            o_scr[sl, cols] = intra + lax.dot_general(qs, st.astype(BF16), _NT, preferred_element_type=F32)
            st = st * decay + inc
        st_scr[j] = st
        finals.append(st)
        o_ref[:, cols] = _hgrn_finish(o_scr[:, cols], g_ref[:, cols], gn_ref[...])

    @pl.when(t == pl.num_programs(2) - 1)
    def _():
        for j, st in enumerate(finals):
            sfin_ref[j] = st.T


def _hgrn_prompt(z, lb_raw, gn, kv_stacked, *, batch, seq, heads, layer, chunk, tt, hp):
    nt = seq // tt
    groups = heads // hp
    w = hp * LANES
    hw = heads * LANES
    n_steps = batch * groups * nt
    kr = batch * seq // n_steps
    kcol, vcol = 5, 6
    step = lambda b, h, t: (b * groups + h) * nt + t
    col = lambda part: (lambda b, h, t: (b * nt + t, part * groups + h))
    kern = functools.partial(_hgrn_prompt_kernel, chunk=chunk, layer=layer, heads=heads)
    depth = lb_raw.shape[0]
    rows_shape = jax.ShapeDtypeStruct((depth, batch * seq * heads, LANES), F32)
    rows_spec = pl.BlockSpec((None, kr * heads, LANES), lambda b, h, t: (layer, step(b, h, t), 0))
    in_specs = [pl.BlockSpec((depth, w), lambda b, h, t: (0, h)),
                pl.BlockSpec((tt, w), col(0)),
                pl.BlockSpec((tt, w), col(1)),
                pl.BlockSpec((tt, w), col(2)),
                pl.BlockSpec((tt, w), col(3)),
                pl.BlockSpec((1, LANES), lambda b, h, t: (0, 0)),
                pl.BlockSpec((kr, hw), lambda b, h, t: (step(b, h, t), kcol)),
                pl.BlockSpec((kr, hw), lambda b, h, t: (step(b, h, t), vcol))]
    args = [lb_raw, z, z, z, z, gn, z, z]
    aliases = {}
    if kv_stacked is not None:
        in_specs += [pl.BlockSpec(memory_space=pl.ANY)] * 2
        aliases = {len(args): 2, len(args) + 1: 3}
        args += list(kv_stacked)
    o, sfin, k_rows, v_rows = pl.pallas_call(
        kern,
        out_shape=(jax.ShapeDtypeStruct((batch * seq, hw), BF16),
                   jax.ShapeDtypeStruct((batch, heads, LANES, LANES), F32),
                   rows_shape, rows_shape),
        grid=(batch, groups, nt),
        in_specs=in_specs,
        out_specs=(pl.BlockSpec((tt, w), lambda b, h, t: (b * nt + t, h)),
                   pl.BlockSpec((None, hp, LANES, LANES), lambda b, h, t: (b, h, 0, 0)),
                   rows_spec, rows_spec),
        scratch_shapes=[pltpu.VMEM((hp, LANES, LANES), F32), pltpu.VMEM((tt, w), F32)],
        input_output_aliases=aliases,
        compiler_params=_params("arbitrary", "arbitrary", "arbitrary"),
        name="hgrn_prompt",
    )(*args)
    return o, sfin, (k_rows, v_rows)


def _hgrn_sample_kernel(lb_ref, q_ref, f_ref, i_ref, g_ref, gn_ref, s0_ref, *rest, chunk, layer):
    o_ref, sfin_ref, o_scr = rest[-3:]
    lb = _lower_bound(lb_ref[...], layer)
    q, k, b = _hgrn_gates(lb, q_ref[...], f_ref[...], chunk)
    for s, (sl, intra, qs, decay, inc) in enumerate(_hgrn_tile(q, k, i_ref[...], b, chunk)):
        st = s0_ref[s].T
        o_scr[sl, :] = intra + lax.dot_general(qs, st.astype(BF16), _NT, preferred_element_type=F32)
        sfin_ref[s] = (st * decay + inc).T
    o_ref[...] = _hgrn_finish(o_scr[...], g_ref[...], gn_ref[...])


def _hgrn_sample(z, lb_raw, gn, state, stacked, *, row0, nseq, steps, heads, layer, bb):
    rows = bb * steps
    rb0 = row0 // rows
    col = lambda base: (lambda i, h: (rb0 + i, base + h))
    kern = functools.partial(_hgrn_sample_kernel, chunk=steps, layer=layer)
    depth = lb_raw.shape[0]
    state_spec = pl.BlockSpec((None, bb, None, LANES, LANES), lambda i, h: (layer, i, h, 0, 0))
    in_specs = [pl.BlockSpec((depth, LANES), lambda i, h: (0, h)),
                pl.BlockSpec((rows, LANES), col(0)),
                pl.BlockSpec((rows, LANES), col(heads)),
                pl.BlockSpec((rows, LANES), col(2 * heads)),
                pl.BlockSpec((rows, LANES), col(3 * heads)),
                pl.BlockSpec((1, LANES), lambda i, h: (0, 0)),
                state_spec]
    args = [lb_raw, z, z, z, z, gn, state]
    aliases = {}
    if stacked is not None:
        in_specs.append(pl.BlockSpec(memory_space=pl.ANY))
        aliases = {len(args): 1}
        args.append(stacked)
    return pl.pallas_call(
        kern,
        out_shape=(jax.ShapeDtypeStruct((nseq * steps, heads * LANES), BF16),
                   jax.ShapeDtypeStruct(state.shape, F32)),
        grid=(nseq // bb, heads),
        in_specs=in_specs,
        out_specs=(pl.BlockSpec((rows, LANES), lambda i, h: (i, h)), state_spec),
        scratch_shapes=[pltpu.VMEM((rows, LANES), F32)],
        input_output_aliases=aliases,
        compiler_params=_params("arbitrary", "arbitrary"),
        name="hgrn_sample",
    )(*args)


def _fox_prompt_kernel(q_ref, k_ref, v_ref, c_ref, o_ref, vt_scr, *, tq, scale):
    h = pl.program_id(1)
    qi = pl.program_id(2)

    @pl.when(qi == 0)
    def _():
        vt_scr[...] = v_ref[...].T.astype(BF16)

    q_t = (q_ref[...] * (scale * LOG2E)).T.astype(BF16)
    lane = lax.broadcasted_iota(jnp.int32, (tq, LANES), 1)

    def rows(j):
        return pl.ds(pl.multiple_of(j * tq, tq), tq)

    def scores(j):
        kj = k_ref[rows(j), :].astype(BF16)
        c_keys = jnp.sum(jnp.where(lane == h, c_ref[rows(j), :], 0.0), axis=-1, keepdims=True)
        return jnp.dot(kj, q_t, preferred_element_type=F32) - c_keys * LOG2E

    def update(j, s, m, l, acc):
        m_new = jnp.maximum(m, jnp.max(s, axis=0, keepdims=True))
        alpha = jnp.exp2(m - m_new)
        p = jnp.exp2(s - m_new)
        l = alpha * l + jnp.sum(p, axis=0, keepdims=True)
        acc = alpha * acc + jnp.dot(vt_scr[:, rows(j)], p.astype(BF16), preferred_element_type=F32)
        return m_new, l, acc

    init = (jnp.full((1, tq), -jnp.inf, F32), jnp.zeros((1, tq), F32), jnp.zeros((LANES, tq), F32))
    stats = lax.fori_loop(0, qi, lambda j, st: update(j, scores(j), *st), init)
    causal = (lax.broadcasted_iota(jnp.int32, (tq, tq), 0) <= lax.broadcasted_iota(jnp.int32, (tq, tq), 1))
    _, l, acc = update(qi, jnp.where(causal, scores(qi), -jnp.inf), *stats)
    o_ref[...] = (acc / l).T.astype(BF16)


def _fox_prompt(z, c, *, batch, seq, heads, tq):
    nq = seq // tq
    d = LANES
    qcol, kcol, vcol = 4 * heads, 5 * heads, 6 * heads
    kern = functools.partial(_fox_prompt_kernel, tq=tq, scale=d ** -0.5)
    return pl.pallas_call(
        kern,
        out_shape=jax.ShapeDtypeStruct((batch * seq, heads * d), BF16),
        grid=(batch, heads, nq),
        in_specs=[pl.BlockSpec((tq, d), lambda b, h, i: (b * nq + i, qcol + h)),
                  pl.BlockSpec((seq, d), lambda b, h, i: (b, kcol + h)),
                  pl.BlockSpec((seq, d), lambda b, h, i: (b, vcol + h)),
                  pl.BlockSpec((seq, LANES), lambda b, h, i: (b, 0))],
        out_specs=pl.BlockSpec((tq, d), lambda b, h, i: (b * nq + i, h)),
        scratch_shapes=[pltpu.VMEM((d, seq), BF16)],
        compiler_params=_params("arbitrary", "arbitrary", "arbitrary"),
        name="fox_prompt",
    )(z, z, z, c)


def _fox_decode_kernel(pt_ref, q_ref, kn_ref, vn_ref, lfn_ref, *rest, n_pages, heads, steps, page, scale):
    k_refs, v_refs = rest[:n_pages], rest[n_pages:2 * n_pages]
    lf_ref, o_ref = rest[2 * n_pages:]
    d = LANES
    q = q_ref[...] * scale

    seq_id = pl.program_id(0)
    lf_pages = [lf_ref[pt_ref[seq_id, i]] for i in range(n_pages)]
    pad = lambda a: jnp.concatenate([a, jnp.zeros((page - steps, a.shape[1]), a.dtype)], axis=0)
    lf_pages.append(pad(lfn_ref[...]).T[0:heads, :])
    c_pages, before = [], jnp.zeros((heads, 1), F32)
    for lf in lf_pages:
        c_pages.append(_cumsum_lanes(lf) + before)
        before = before + jnp.sum(lf, axis=-1, keepdims=True)

    qpos = lax.broadcasted_iota(jnp.int32, (steps, page), 0)
    kpos = lax.broadcasted_iota(jnp.int32, (steps, page), 1)
    causal = kpos <= qpos

    def scores(h):
        cols = slice(h * d, (h + 1) * d)
        qh = q[:, cols].astype(BF16)
        keys = [r[pl.ds(h, page, stride=heads), :] for r in k_refs] + [pad(kn_ref[:, cols])]
        s = [lax.dot_general(qh, k.astype(BF16), _NT, preferred_element_type=F32) - c[h:h + 1, :]
             for k, c in zip(keys, c_pages)]
        s[-1] = jnp.where(causal, s[-1], -jnp.inf)
        return s

    def attend(h, s):
        cols = slice(h * d, (h + 1) * d)
        vals = [r[pl.ds(h, page, stride=heads), :] for r in v_refs] + [pad(vn_ref[:, cols])]
        m = functools.reduce(jnp.maximum, s)
        m = jnp.max(m, axis=-1, keepdims=True)
        p = [jnp.exp(t - m) for t in s]
        l = jnp.sum(functools.reduce(lambda a, b: a + b, p), axis=-1, keepdims=True)
        acc = functools.reduce(lambda a, b: a + b,
                               [jnp.dot(t.astype(BF16), v.astype(BF16), preferred_element_type=F32)
                                for t, v in zip(p, vals)])
        o_ref[:, cols] = acc / l

    ahead = min(2, heads)
    pending = [scores(h) for h in range(ahead)]
    for h in range(heads):
        if h + ahead < heads:
            pending.append(scores(h + ahead))
        attend(h, pending.pop(0))


def _fox_decode(z, logf, cache_k, cache_v, cache_logf_t, page_table, *, row0, layer, heads, steps):
    nseq, n_pages = page_table.shape
    depth, n_pool, page = cache_k.shape[:3]
    assert page == LANES
    d = LANES
    w = heads * d
    rb0 = row0 // steps
    qcol, kcol, vcol = 4, 5, 6
    ck = cache_k.reshape(depth, n_pool, page * heads, d)
    cv = cache_v.reshape(depth, n_pool, page * heads, d)
    kern = functools.partial(_fox_decode_kernel, n_pages=n_pages, heads=heads, steps=steps, page=page,
                             scale=d ** -0.5)
    pg = lambda i: (lambda b, pt: (layer, pt[b, i], 0, 0))
    kv_spec = [pl.BlockSpec((None, None, page * heads, d), pg(i)) for i in range(n_pages)]
    lf_spec = pl.BlockSpec((None, n_pool, heads, page), lambda b, pt: (layer, 0, 0, 0),
                           pipeline_mode=pl.Buffered(1))
    grid_spec = pltpu.PrefetchScalarGridSpec(
        num_scalar_prefetch=1,
        grid=(nseq,),
        in_specs=[pl.BlockSpec((steps, w), lambda b, pt: (rb0 + b, qcol)),
                  pl.BlockSpec((steps, w), lambda b, pt: (rb0 + b, kcol)),
                  pl.BlockSpec((steps, w), lambda b, pt: (rb0 + b, vcol)),
                  pl.BlockSpec((steps, LANES), lambda b, pt: (rb0 + b, 0))]
                 + kv_spec + kv_spec + [lf_spec],
        out_specs=pl.BlockSpec((steps, w), lambda b, pt: (b, 0)),
    )
    return pl.pallas_call(
        kern,
        out_shape=jax.ShapeDtypeStruct((nseq * steps, w), F32),
        grid_spec=grid_spec,
        compiler_params=_params("arbitrary"),
        name="fox_decode",
    )(page_table, z, z, z, logf, *([ck] * n_pages), *([cv] * n_pages), cache_logf_t)


def _outproj_kernel(ohp_ref, ohs_ref, ofp_ref, ofs_ref, x_ref, wa_ref, wb_ref, npost_ref, npre_ref,
                    x1_ref, h2_ref, *, prompt_tiles):
    is_prompt = pl.program_id(0) < prompt_tiles
    o_hg = jnp.where(is_prompt, ohp_ref[...], ohs_ref[...])
    o_fx = jnp.where(is_prompt, ofp_ref[...], ofs_ref[...].astype(BF16))
    y = (jnp.dot(o_hg, wa_ref[...], preferred_element_type=F32)
         + jnp.dot(o_fx, wb_ref[...], preferred_element_type=F32))
    x1 = x_ref[...] + _rms(y) * npost_ref[...]
    x1_ref[...] = x1
    h2_ref[...] = (_rms(x1) * npre_ref[...]).astype(BF16)


def _outproj(ohg_p, ohg_s, ofx_p, ofx_s, x, w_out, n_post, n_pre, *, layer, tm):
    m, d = x.shape
    wd = ohg_p.shape[1]
    prompt_tiles = ohg_p.shape[0] // tm
    row = lambda i: (i, 0)
    prow = lambda i: (jnp.minimum(i, prompt_tiles - 1), 0)
    srow = lambda i: (jnp.maximum(i - prompt_tiles, 0), 0)
    fixed = lambda i: (0, 0)
    return pl.pallas_call(
        functools.partial(_outproj_kernel, prompt_tiles=prompt_tiles),
        out_shape=(jax.ShapeDtypeStruct((m, d), F32), jax.ShapeDtypeStruct((m, d), BF16)),
        grid=(m // tm,),
        in_specs=[pl.BlockSpec((tm, wd), prow), pl.BlockSpec((tm, wd), srow),
                  pl.BlockSpec((tm, wd), prow), pl.BlockSpec((tm, wd), srow),
                  pl.BlockSpec((tm, d), row),
                  pl.BlockSpec((None, wd, d), lambda i: (layer, 0, 0)),
                  pl.BlockSpec((None, wd, d), lambda i: (layer, 1, 0)),
                  pl.BlockSpec((1, d), fixed), pl.BlockSpec((1, d), fixed)],
        out_specs=(pl.BlockSpec((tm, d), row), pl.BlockSpec((tm, d), row)),
        compiler_params=_params("arbitrary"),
        name="outproj",
    )(ohg_p, ohg_s, ofx_p, ofx_s, x, w_out, w_out, n_post, n_pre)


def _ffn_kernel(h_ref, x_ref, wg_ref, wu_ref, wd_ref, npost_ref, o_ref):
    j = pl.program_id(1)

    @pl.when(j == 0)
    def _():
        o_ref[...] = jnp.zeros_like(o_ref)

    h = h_ref[...]
    a = jnp.dot(h, wg_ref[...], preferred_element_type=F32)
    b = jnp.dot(h, wu_ref[...], preferred_element_type=F32)
    sig_a, _, _ = _sigmoid_parts(a)
    o_ref[...] += jnp.dot((a * sig_a * b).astype(BF16), wd_ref[...], preferred_element_type=F32)

    @pl.when(j == pl.num_programs(1) - 1)
    def _():
        o_ref[...] = x_ref[...] + _rms(o_ref[...]) * npost_ref[...]


def _ffn(h2, x1, w_g, w_u, w_d, n_post, *, layer, tm, tf):
    m, d = x1.shape
    f = w_g.shape[2]
    return pl.pallas_call(
        _ffn_kernel,
        out_shape=jax.ShapeDtypeStruct((m, d), F32),
        grid=(m // tm, f // tf),
        in_specs=[pl.BlockSpec((tm, d), lambda i, j: (i, 0)),
                  pl.BlockSpec((tm, d), lambda i, j: (i, 0)),
                  pl.BlockSpec((None, d, tf), lambda i, j: (layer, 0, j)),
                  pl.BlockSpec((None, d, tf), lambda i, j: (layer, 0, j)),
                  pl.BlockSpec((None, tf, d), lambda i, j: (layer, j, 0)),
                  pl.BlockSpec((1, d), lambda i, j: (0, 0))],
        out_specs=pl.BlockSpec((tm, d), lambda i, j: (i, 0)),
        compiler_params=_params("arbitrary", "arbitrary"),
        name="ffn",
    )(h2, x1, w_g, w_u, w_d, n_post)


def _tiles(batch, seq, nseq, steps, d_ff):
    m = batch * seq + nseq * steps
    tm = next(t for t in (1024, 512, 256, 128) if seq % t == 0 and m % t == 0)
    tm_out = next(t for t in (512, 256, 128) if (batch * seq) % t == 0 and (nseq * steps) % t == 0)
    tm_ffn = next(t for t in (768, 512, 256, 128) if m % t == 0)
    tf = next(t for t in (512, 256, 128) if d_ff % t == 0)
    tt = next(t for t in (512, 256, 128, HGRN_CHUNK) if seq % t == 0)
    tq = next(t for t in (1024, 512, 256, 128) if seq % t == 0)
    bb = next(b for b in (32, 16, 8, 4, 2, 1) if nseq % b == 0 and (batch * seq) % (b * steps) == 0)
    return dict(tm=tm, tn=1024, tm_out=tm_out, tm_ffn=tm_ffn, tf=tf, tt=tt, tq=tq, bb=bb, hp=4)


def kernel(x_prompt, x_sample, cache_k, cache_v, cache_logf, state_hgrn, page_table, w_in, b_fox_f,
           hg_lower_bounds, hg_norm_w, w_out, norm_mix_pre, norm_mix_post, norm_ffn_pre, norm_ffn_post,
           w_gate, w_up, w_down):
    batch, seq, d = x_prompt.shape
    nseq, steps, _ = x_sample.shape
    depth, n_pool, page, fh, fd = cache_k.shape
    hh, hk, hv = state_hgrn.shape[2:]
    d_ff = w_gate.shape[2]
    assert hk == LANES and hv == LANES and fd == LANES and hh == fh
    heads = hh
    n_main = 7 * heads * LANES
    assert w_in.shape[2] == n_main + fh and fh <= LANES
    mp = batch * seq
    t = _tiles(batch, seq, nseq, steps, d_ff)
    assert seq % 16 == 0

    x = jnp.concatenate([x_prompt.reshape(mp, d), x_sample.reshape(nseq * steps, d)], axis=0)
    row = lambda a: a.reshape(1, -1).astype(F32)
    cache_logf_t = cache_logf.transpose(0, 1, 3, 2)
    w_in_b, w_out_b = w_in.astype(BF16), w_out.astype(BF16)
    w_gate_b, w_up_b, w_down_b = w_gate.astype(BF16), w_up.astype(BF16), w_down.astype(BF16)

    outs = {n: [] for n in ("gp", "sp", "ks", "vs", "gs")}
    hgrn_sample_states = kv_prompt = None
    for l in range(depth):
        w_ff = jnp.pad(w_in[l, :, n_main:], ((0, 0), (0, LANES - fh))).astype(BF16)
        b_ff = jnp.pad(b_fox_f[l].astype(F32), (0, LANES - fh)).reshape(1, LANES)
        z, logf, c = _inproj(x, row(norm_mix_pre[l]), w_in_b, w_ff, b_ff, layer=l, n=n_main, tm=t["tm"],
                             tn=t["tn"], seq_len=seq)

        gn = row(hg_norm_w[l])
        lb_raw = hg_lower_bounds.astype(F32)
        ohg_p, s_p, kv_prompt = _hgrn_prompt(z, lb_raw, gn, kv_prompt, batch=batch, seq=seq, heads=heads,
                                             layer=l, chunk=HGRN_CHUNK, tt=t["tt"], hp=t["hp"])
        ohg_s, hgrn_sample_states = _hgrn_sample(z, lb_raw, gn, state_hgrn, hgrn_sample_states, row0=mp,
                                                 nseq=nseq, steps=steps, heads=heads, layer=l, bb=t["bb"])

        ofx_p = _fox_prompt(z, c, batch=batch, seq=seq, heads=heads, tq=t["tq"])
        ofx_s = _fox_decode(z, logf, cache_k, cache_v, cache_logf_t, page_table, row0=mp, layer=l, heads=heads,
                            steps=steps)

        hw = heads * LANES
        x1, h2 = _outproj(ohg_p, ohg_s, ofx_p, ofx_s, x, w_out_b, row(norm_mix_post[l]), row(norm_ffn_pre[l]),
                          layer=l, tm=t["tm_out"])
        x = _ffn(h2, x1, w_gate_b, w_up_b, w_down_b, row(norm_ffn_post[l]), layer=l,
                 tm=t["tm_ffn"], tf=t["tf"])

        kcol, vcol = 5 * heads * LANES, 6 * heads * LANES
        outs["gp"].append(logf[:mp, :fh].reshape(batch, seq, fh))
        outs["sp"].append(s_p)
        outs["ks"].append(z[mp:, kcol:kcol + hw].reshape(nseq, steps, fh, fd))
        outs["vs"].append(z[mp:, vcol:vcol + hw].reshape(nseq, steps, fh, fd))
        outs["gs"].append(logf[mp:, :fh].reshape(nseq, steps, fh))

    st = {n: jnp.stack(v) for n, v in outs.items()}
    k_prompt, v_prompt = (a.reshape(depth, batch, seq, fh, fd) for a in kv_prompt)
    return (x[:mp].reshape(batch, seq, d), x[mp:].reshape(nseq, steps, d),
            k_prompt, v_prompt, st["gp"], st["sp"], st["ks"], st["vs"], st["gs"], hgrn_sample_states)
```

```python
import functools

import jax
import jax.numpy as jnp
from jax import lax
from jax.experimental import pallas as pl
from jax.experimental.pallas import tpu as pltpu

NORM_EPS = 1e-6
LOG2E = 1.4426950408889634
LANES = 128
SUBLANES = 8
VMEM_LIMIT_BYTES = 56 * 1024 * 1024
HGRN_CHUNK = 16
HGRN_ATT_ROWS = 256
CAST_SPLIT = 8

F32 = jnp.float32
BF16 = jnp.bfloat16

_NT = (((1,), (1,)), ((), ()))
_TN = (((0,), (0,)), ((), ()))


def _params(*sem):
    return pltpu.CompilerParams(dimension_semantics=sem, vmem_limit_bytes=VMEM_LIMIT_BYTES)


def _rms(x):
    return x * lax.rsqrt(jnp.mean(x * x, axis=-1, keepdims=True) + NORM_EPS)


def _sigmoid_parts(z):
    e = jnp.exp(-jnp.abs(z))
    r = 1.0 / (1.0 + e)
    pos = z >= 0
    return jnp.where(pos, r, e * r), jnp.where(pos, e * r, r), jnp.minimum(z, 0.0) - jnp.log1p(e)


def _cumsum_rows(x, period=None):
    n = x.shape[0]
    span = n if period is None else period
    row = lax.broadcasted_iota(jnp.int32, x.shape, 0)
    if period is not None:
        row = row % period
    s = 1
    while s < span:
        x = x + jnp.where(row >= s, pltpu.roll(x, s, axis=0), 0.0)
        s *= 2
    return x


def _cumsum_lanes(x):
    n = x.shape[-1]
    lane = lax.broadcasted_iota(jnp.int32, x.shape, x.ndim - 1)
    s = 1
    while s < n:
        x = x + jnp.where(lane >= s, pltpu.roll(x, s, axis=x.ndim - 1), 0.0)
        s *= 2
    return x


def _inproj_kernel(x_ref, nw_ref, w_ref, wff_ref, bf_ref, z_ref, logf_ref, c_ref,
                   h_scr, carry_scr, *, seq_tiles):
    i = pl.program_id(0)
    j = pl.program_id(1)

    @pl.when(j == 0)
    def _():
        hb = (_rms(x_ref[...]) * nw_ref[...]).astype(BF16)
        h_scr[...] = hb
        ff = jnp.dot(hb, wff_ref[...], preferred_element_type=F32) + bf_ref[...]
        _, _, lf = _sigmoid_parts(ff)
        logf_ref[...] = lf

        @pl.when(i % seq_tiles == 0)
        def _():
            carry_scr[...] = jnp.zeros_like(carry_scr)

        c = _cumsum_rows(lf) + carry_scr[...]
        c_ref[...] = c
        carry_scr[...] = c[c.shape[0] - 1:, :]

    z_ref[...] = jnp.dot(h_scr[...], w_ref[...], preferred_element_type=F32)


def _inproj(x, nw, w_in, w_ff, b_ff, *, layer, n, tm, tn, seq_len):
    m, d = x.shape
    kern = functools.partial(_inproj_kernel, seq_tiles=seq_len // tm)
    return pl.pallas_call(
        kern,
        out_shape=(jax.ShapeDtypeStruct((m, n), F32),
                   jax.ShapeDtypeStruct((m, LANES), F32),
                   jax.ShapeDtypeStruct((m, LANES), F32)),
        grid=(m // tm, n // tn),
        in_specs=[pl.BlockSpec((tm, d), lambda i, j: (i, 0)),
                  pl.BlockSpec((1, d), lambda i, j: (0, 0)),
                  pl.BlockSpec((None, d, tn), lambda i, j: (layer, 0, j)),
                  pl.BlockSpec((d, LANES), lambda i, j: (0, 0)),
                  pl.BlockSpec((1, LANES), lambda i, j: (0, 0))],
        out_specs=(pl.BlockSpec((tm, tn), lambda i, j: (i, j)),
                   pl.BlockSpec((tm, LANES), lambda i, j: (i, 0)),
                   pl.BlockSpec((tm, LANES), lambda i, j: (i, 0))),
        scratch_shapes=[pltpu.VMEM((tm, d), BF16), pltpu.VMEM((1, LANES), F32)],
        compiler_params=_params("arbitrary", "arbitrary"),
        name="inproj",
    )(x, nw, w_in, w_ff, b_ff)


def _lower_bound(raw, layer):
    depth = raw.shape[0]
    rows = [raw[i:i + 1, :] for i in range(depth)]
    mx = functools.reduce(jnp.maximum, rows)
    ex = [jnp.exp(r - mx) for r in rows]
    tot = functools.reduce(lambda a, b: a + b, ex)
    p = [e / tot for e in ex]
    cum = functools.reduce(lambda a, b: a + b, p[:layer + 1])
    return cum - p[0]


def _hgrn_gates(lb, qr, z, chunk):
    sig_q, _, _ = _sigmoid_parts(qr)
    q = qr * sig_q
    _, sig_nz, logsig = _sigmoid_parts(z)
    a = jnp.log(lb)
    bb = jnp.log1p(-lb) + logsig
    logf = jnp.maximum(a, bb) + jnp.log1p(jnp.exp(-jnp.abs(a - bb)))
    k = (1.0 - lb) * sig_nz
    return q, k, _cumsum_rows(logf, period=chunk)


def _hgrn_tile(q, k, v, b, chunk, pair=False):
    rows = q.shape[0]
    chunks = [slice(c * chunk, (c + 1) * chunk) for c in range(rows // chunk)]
    spread = lambda r: jnp.broadcast_to(r, (chunk, r.shape[1]))
    ref = jnp.concatenate([spread(b[s.start + chunk // 2 - 1:s.start + chunk // 2]) for s in chunks], axis=0)
    ends = [b[s.stop - 1:s.stop] for s in chunks]
    end = jnp.concatenate([spread(e) for e in ends], axis=0)
    vb = v.astype(BF16)
    qi = (q * jnp.exp(b - ref)).astype(BF16)
    ki = (k * jnp.exp(ref - b)).astype(BF16)
    g = min(rows, HGRN_ATT_ROWS)
    r = lax.broadcasted_iota(jnp.int32, (g, g), 0)
    c = lax.broadcasted_iota(jnp.int32, (g, g), 1)
    keep = (jnp.bitwise_and(r, -chunk) == jnp.bitwise_and(c, -chunk)) & (r >= c)
    qs = q * jnp.exp(b)
    ks = k * jnp.exp(end - b)
    if pair:
        second_on_first = ((jnp.bitwise_and(r, -2 * chunk) == jnp.bitwise_and(c, -2 * chunk))
                           & (jnp.bitwise_and(r, chunk) != 0) & (jnp.bitwise_and(c, chunk) == 0))
        qs_b, ks_b = qs.astype(BF16), ks.astype(BF16)
    parts = []
    for s in (slice(i, i + g) for i in range(0, rows, g)):
        att = jnp.where(keep, lax.dot_general(qi[s], ki[s], _NT, preferred_element_type=F32), 0.0)
        if pair:
            cross = lax.dot_general(qs_b[s], ks_b[s], _NT, preferred_element_type=F32)
            att = jnp.where(second_on_first, cross, att)
        parts.append(jnp.dot(att.astype(BF16), vb[s], preferred_element_type=F32))
    intra = jnp.concatenate(parts, axis=0)
    incs = [lax.dot_general(v[s].astype(BF16), ks[s].astype(BF16), _TN, preferred_element_type=F32)
            for s in chunks]
    if not pair:
        return [(s, intra[s], qs[s].astype(BF16), jnp.exp(e), u) for s, e, u in zip(chunks, ends, incs)]
    steps = []
    for c0 in range(0, len(chunks), 2):
        s0, s1 = chunks[c0], chunks[c0 + 1]
        d0, d1 = jnp.exp(ends[c0]), jnp.exp(ends[c0 + 1])
        both = slice(s0.start, s1.stop)
        reads = jnp.concatenate([qs[s0], qs[s1] * d0], axis=0).astype(BF16)
        steps.append((both, intra[both], reads, d0 * d1, incs[c0] * d1 + incs[c0 + 1]))
    return steps


def _hgrn_finish(o, g, gn):
    sig_g, _, _ = _sigmoid_parts(g)
    return (_rms(o) * gn * (g * sig_g)).astype(BF16)


def _hgrn_prompt_kernel(lb_ref, q_ref, f_ref, i_ref, g_ref, gn_ref, kz_ref, vz_ref, *rest,
                        chunk, layer, heads):
    o_ref, sfin_ref, ko_ref, vo_ref, st_scr, o_scr = rest[-6:]
    t = pl.program_id(2)

    @pl.when(t == 0)
    def _():
        st_scr[...] = jnp.zeros_like(st_scr)

    kv_rows = kz_ref.shape[0]
    for src, dst in ((kz_ref, ko_ref), (vz_ref, vo_ref)):
        for h in range(heads):
            dst[pl.ds(h, kv_rows, stride=heads), :] = src[:, h * LANES:(h + 1) * LANES]

    finals = []
    for j in range(st_scr.shape[0]):
        cols = slice(j * LANES, (j + 1) * LANES)
        lb = _lower_bound(lb_ref[:, cols], layer)
        q, k, b = _hgrn_gates(lb, q_ref[:, cols], f_ref[:, cols], chunk)
        st = st_scr[j]
        for sl, intra, qs, decay, inc in _hgrn_tile(q, k, i_ref[:, cols], b, chunk, pair=True):
            o_scr[sl, cols] = intra + lax.dot_general(qs, st.astype(BF16), _NT, preferred_element_type=F32)
            st = st * decay + inc
        st_scr[j] = st
        finals.append(st)
        o_ref[:, cols] = _hgrn_finish(o_scr[:, cols], g_ref[:, cols], gn_ref[...])

    @pl.when(t == pl.num_programs(2) - 1)
    def _():
        for j, st in enumerate(finals):
            sfin_ref[j] = st.T


def _hgrn_prompt(z, lb_raw, gn, kv_stacked, *, batch, seq, heads, layer, chunk, tt, hp):
    nt = seq // tt
    groups = heads // hp
    w = hp * LANES
    hw = heads * LANES
    n_steps = batch * groups * nt
    kr = batch * seq // n_steps
    kcol, vcol = 5, 6
    step = lambda b, h, t: (b * groups + h) * nt + t
    col = lambda part: (lambda b, h, t: (b * nt + t, part * groups + h))
    kern = functools.partial(_hgrn_prompt_kernel, chunk=chunk, layer=layer, heads=heads)
    depth = lb_raw.shape[0]
    rows_shape = jax.ShapeDtypeStruct((depth, batch * seq * heads, LANES), F32)
    rows_spec = pl.BlockSpec((None, kr * heads, LANES), lambda b, h, t: (layer, step(b, h, t), 0))
    in_specs = [pl.BlockSpec((depth, w), lambda b, h, t: (0, h)),
                pl.BlockSpec((tt, w), col(0)),
                pl.BlockSpec((tt, w), col(1)),
                pl.BlockSpec((tt, w), col(2)),
                pl.BlockSpec((tt, w), col(3)),
                pl.BlockSpec((1, LANES), lambda b, h, t: (0, 0)),
                pl.BlockSpec((kr, hw), lambda b, h, t: (step(b, h, t), kcol)),
                pl.BlockSpec((kr, hw), lambda b, h, t: (step(b, h, t), vcol))]
    args = [lb_raw, z, z, z, z, gn, z, z]
    aliases = {}
    if kv_stacked is not None:
        in_specs += [pl.BlockSpec(memory_space=pl.ANY)] * 2
        aliases = {len(args): 2, len(args) + 1: 3}
        args += list(kv_stacked)
    o, sfin, k_rows, v_rows = pl.pallas_call(
        kern,
        out_shape=(jax.ShapeDtypeStruct((batch * seq, hw), BF16),
                   jax.ShapeDtypeStruct((batch, heads, LANES, LANES), F32),
                   rows_shape, rows_shape),
        grid=(batch, groups, nt),
        in_specs=in_specs,
        out_specs=(pl.BlockSpec((tt, w), lambda b, h, t: (b * nt + t, h)),
                   pl.BlockSpec((None, hp, LANES, LANES), lambda b, h, t: (b, h, 0, 0)),
                   rows_spec, rows_spec),
        scratch_shapes=[pltpu.VMEM((hp, LANES, LANES), F32), pltpu.VMEM((tt, w), F32)],
        input_output_aliases=aliases,
        compiler_params=_params("arbitrary", "arbitrary", "arbitrary"),
        name="hgrn_prompt",
    )(*args)
    return o, sfin, (k_rows, v_rows)


def _hgrn_sample_kernel(lb_ref, q_ref, f_ref, i_ref, g_ref, gn_ref, s0_ref, *rest, chunk, layer):
    o_ref, sfin_ref, o_scr = rest[-3:]
    lb = _lower_bound(lb_ref[...], layer)
    q, k, b = _hgrn_gates(lb, q_ref[...], f_ref[...], chunk)
    for s, (sl, intra, qs, decay, inc) in enumerate(_hgrn_tile(q, k, i_ref[...], b, chunk)):
        st = s0_ref[s].T
        o_scr[sl, :] = intra + lax.dot_general(qs, st.astype(BF16), _NT, preferred_element_type=F32)
        sfin_ref[s] = (st * decay + inc).T
    o_ref[...] = _hgrn_finish(o_scr[...], g_ref[...], gn_ref[...])


def _hgrn_sample(z, lb_raw, gn, state, stacked, *, row0, nseq, steps, heads, layer, bb):
    rows = bb * steps
    rb0 = row0 // rows
    col = lambda base: (lambda i, h: (rb0 + i, base + h))
    kern = functools.partial(_hgrn_sample_kernel, chunk=steps, layer=layer)
    depth = lb_raw.shape[0]
    state_spec = pl.BlockSpec((None, bb, None, LANES, LANES), lambda i, h: (layer, i, h, 0, 0))
    in_specs = [pl.BlockSpec((depth, LANES), lambda i, h: (0, h)),
                pl.BlockSpec((rows, LANES), col(0)),
                pl.BlockSpec((rows, LANES), col(heads)),
                pl.BlockSpec((rows, LANES), col(2 * heads)),
                pl.BlockSpec((rows, LANES), col(3 * heads)),
                pl.BlockSpec((1, LANES), lambda i, h: (0, 0)),
                state_spec]
    args = [lb_raw, z, z, z, z, gn, state]
    aliases = {}
    if stacked is not None:
        in_specs.append(pl.BlockSpec(memory_space=pl.ANY))
        aliases = {len(args): 1}
        args.append(stacked)
    return pl.pallas_call(
        kern,
        out_shape=(jax.ShapeDtypeStruct((nseq * steps, heads * LANES), BF16),
                   jax.ShapeDtypeStruct(state.shape, F32)),
        grid=(nseq // bb, heads),
        in_specs=in_specs,
        out_specs=(pl.BlockSpec((rows, LANES), lambda i, h: (i, h)), state_spec),
        scratch_shapes=[pltpu.VMEM((rows, LANES), F32)],
        input_output_aliases=aliases,
        compiler_params=_params("arbitrary", "arbitrary"),
        name="hgrn_sample",
    )(*args)


def _fox_prompt_kernel(q_ref, k_ref, v_ref, c_ref, o_ref, vt_scr, *, tq, scale):
    h = pl.program_id(1)
    qi = pl.program_id(2)

    @pl.when(qi == 0)
    def _():
        vt_scr[...] = v_ref[...].T.astype(BF16)

    q_t = (q_ref[...] * (scale * LOG2E)).T.astype(BF16)
    lane = lax.broadcasted_iota(jnp.int32, (tq, LANES), 1)

    def rows(j):
        return pl.ds(pl.multiple_of(j * tq, tq), tq)

    def scores(j):
        kj = k_ref[rows(j), :].astype(BF16)
        c_keys = jnp.sum(jnp.where(lane == h, c_ref[rows(j), :], 0.0), axis=-1, keepdims=True)
        return jnp.dot(kj, q_t, preferred_element_type=F32) - c_keys * LOG2E

    def update(j, s, m, l, acc):
        m_new = jnp.maximum(m, jnp.max(s, axis=0, keepdims=True))
        alpha = jnp.exp2(m - m_new)
        p = jnp.exp2(s - m_new)
        l = alpha * l + jnp.sum(p, axis=0, keepdims=True)
        acc = alpha * acc + jnp.dot(vt_scr[:, rows(j)], p.astype(BF16), preferred_element_type=F32)
        return m_new, l, acc

    init = (jnp.full((1, tq), -jnp.inf, F32), jnp.zeros((1, tq), F32), jnp.zeros((LANES, tq), F32))
    stats = lax.fori_loop(0, qi, lambda j, st: update(j, scores(j), *st), init)
    causal = (lax.broadcasted_iota(jnp.int32, (tq, tq), 0) <= lax.broadcasted_iota(jnp.int32, (tq, tq), 1))
    _, l, acc = update(qi, jnp.where(causal, scores(qi), -jnp.inf), *stats)
    o_ref[...] = (acc / l).T.astype(BF16)


def _fox_prompt(z, c, *, batch, seq, heads, tq):
    nq = seq // tq
    d = LANES
    qcol, kcol, vcol = 4 * heads, 5 * heads, 6 * heads
    kern = functools.partial(_fox_prompt_kernel, tq=tq, scale=d ** -0.5)
    return pl.pallas_call(
        kern,
        out_shape=jax.ShapeDtypeStruct((batch * seq, heads * d), BF16),
        grid=(batch, heads, nq),
        in_specs=[pl.BlockSpec((tq, d), lambda b, h, i: (b * nq + i, qcol + h)),
                  pl.BlockSpec((seq, d), lambda b, h, i: (b, kcol + h)),
                  pl.BlockSpec((seq, d), lambda b, h, i: (b, vcol + h)),
                  pl.BlockSpec((seq, LANES), lambda b, h, i: (b, 0))],
        out_specs=pl.BlockSpec((tq, d), lambda b, h, i: (b * nq + i, h)),
        scratch_shapes=[pltpu.VMEM((d, seq), BF16)],
        compiler_params=_params("arbitrary", "arbitrary", "arbitrary"),
        name="fox_prompt",
    )(z, z, z, c)


def _fox_decode_kernel(pt_ref, q_ref, kn_ref, vn_ref, lfn_ref, *rest, n_pages, heads, steps, page, scale):
    k_refs, v_refs = rest[:n_pages], rest[n_pages:2 * n_pages]
    lf_ref, o_ref = rest[2 * n_pages:]
    d = LANES
    q = q_ref[...] * scale

    seq_id = pl.program_id(0)
    lf_pages = [lf_ref[pt_ref[seq_id, i]] for i in range(n_pages)]
    pad = lambda a: jnp.concatenate([a, jnp.zeros((page - steps, a.shape[1]), a.dtype)], axis=0)
    lf_pages.append(pad(lfn_ref[...]).T[0:heads, :])
    c_pages, before = [], jnp.zeros((heads, 1), F32)
    for lf in lf_pages:
        c_pages.append(_cumsum_lanes(lf) + before)
        before = before + jnp.sum(lf, axis=-1, keepdims=True)

    qpos = lax.broadcasted_iota(jnp.int32, (steps, page), 0)
    kpos = lax.broadcasted_iota(jnp.int32, (steps, page), 1)
    causal = kpos <= qpos

    def scores(h):
        cols = slice(h * d, (h + 1) * d)
        qh = q[:, cols].astype(BF16)
        keys = [r[pl.ds(h, page, stride=heads), :] for r in k_refs] + [pad(kn_ref[:, cols])]
        s = [lax.dot_general(qh, k.astype(BF16), _NT, preferred_element_type=F32) - c[h:h + 1, :]
             for k, c in zip(keys, c_pages)]
        s[-1] = jnp.where(causal, s[-1], -jnp.inf)
        return s

    def attend(h, s):
        cols = slice(h * d, (h + 1) * d)
        vals = [r[pl.ds(h, page, stride=heads), :] for r in v_refs] + [pad(vn_ref[:, cols])]
        m = functools.reduce(jnp.maximum, s)
        m = jnp.max(m, axis=-1, keepdims=True)
        p = [jnp.exp(t - m) for t in s]
        l = jnp.sum(functools.reduce(lambda a, b: a + b, p), axis=-1, keepdims=True)
        acc = functools.reduce(lambda a, b: a + b,
                               [jnp.dot(t.astype(BF16), v.astype(BF16), preferred_element_type=F32)
                                for t, v in zip(p, vals)])
        o_ref[:, cols] = acc / l

    ahead = min(2, heads)
    pending = [scores(h) for h in range(ahead)]
    for h in range(heads):
        if h + ahead < heads:
            pending.append(scores(h + ahead))
        attend(h, pending.pop(0))


def _fox_decode(z, logf, cache_k, cache_v, cache_logf_t, page_table, *, row0, layer, heads, steps):
    nseq, n_pages = page_table.shape
    depth, n_pool, page = cache_k.shape[:3]
    assert page == LANES
    d = LANES
    w = heads * d
    rb0 = row0 // steps
    qcol, kcol, vcol = 4, 5, 6
    ck = cache_k.reshape(depth, n_pool, page * heads, d)
    cv = cache_v.reshape(depth, n_pool, page * heads, d)
    kern = functools.partial(_fox_decode_kernel, n_pages=n_pages, heads=heads, steps=steps, page=page,
                             scale=d ** -0.5)
    pg = lambda i: (lambda b, pt: (layer, pt[b, i], 0, 0))
    kv_spec = [pl.BlockSpec((None, None, page * heads, d), pg(i)) for i in range(n_pages)]
    lf_spec = pl.BlockSpec((None, n_pool, heads, page), lambda b, pt: (layer, 0, 0, 0),
                           pipeline_mode=pl.Buffered(1))
    grid_spec = pltpu.PrefetchScalarGridSpec(
        num_scalar_prefetch=1,
        grid=(nseq,),
        in_specs=[pl.BlockSpec((steps, w), lambda b, pt: (rb0 + b, qcol)),
                  pl.BlockSpec((steps, w), lambda b, pt: (rb0 + b, kcol)),
                  pl.BlockSpec((steps, w), lambda b, pt: (rb0 + b, vcol)),
                  pl.BlockSpec((steps, LANES), lambda b, pt: (rb0 + b, 0))]
                 + kv_spec + kv_spec + [lf_spec],
        out_specs=pl.BlockSpec((steps, w), lambda b, pt: (b, 0)),
    )
    return pl.pallas_call(
        kern,
        out_shape=jax.ShapeDtypeStruct((nseq * steps, w), F32),
        grid_spec=grid_spec,
        compiler_params=_params("arbitrary"),
        name="fox_decode",
    )(page_table, z, z, z, logf, *([ck] * n_pages), *([cv] * n_pages), cache_logf_t)


def _outproj_kernel(ohp_ref, ohs_ref, ofp_ref, ofs_ref, x_ref, wa_ref, wb_ref, npost_ref, npre_ref,
                    x1_ref, h2_ref, *, prompt_tiles):
    is_prompt = pl.program_id(0) < prompt_tiles
    o_hg = jnp.where(is_prompt, ohp_ref[...], ohs_ref[...])
    o_fx = jnp.where(is_prompt, ofp_ref[...], ofs_ref[...].astype(BF16))
    y = (jnp.dot(o_hg, wa_ref[...], preferred_element_type=F32)
         + jnp.dot(o_fx, wb_ref[...], preferred_element_type=F32))
    x1 = x_ref[...] + _rms(y) * npost_ref[...]
    x1_ref[...] = x1
    h2_ref[...] = (_rms(x1) * npre_ref[...]).astype(BF16)


def _outproj(ohg_p, ohg_s, ofx_p, ofx_s, x, w_out, n_post, n_pre, *, layer, tm):
    m, d = x.shape
    wd = ohg_p.shape[1]
    prompt_tiles = ohg_p.shape[0] // tm
    row = lambda i: (i, 0)
    prow = lambda i: (jnp.minimum(i, prompt_tiles - 1), 0)
    srow = lambda i: (jnp.maximum(i - prompt_tiles, 0), 0)
    fixed = lambda i: (0, 0)
    return pl.pallas_call(
        functools.partial(_outproj_kernel, prompt_tiles=prompt_tiles),
        out_shape=(jax.ShapeDtypeStruct((m, d), F32), jax.ShapeDtypeStruct((m, d), BF16)),
        grid=(m // tm,),
        in_specs=[pl.BlockSpec((tm, wd), prow), pl.BlockSpec((tm, wd), srow),
                  pl.BlockSpec((tm, wd), prow), pl.BlockSpec((tm, wd), srow),
                  pl.BlockSpec((tm, d), row),
                  pl.BlockSpec((None, wd, d), lambda i: (layer, 0, 0)),
                  pl.BlockSpec((None, wd, d), lambda i: (layer, 1, 0)),
                  pl.BlockSpec((1, d), fixed), pl.BlockSpec((1, d), fixed)],
        out_specs=(pl.BlockSpec((tm, d), row), pl.BlockSpec((tm, d), row)),
        compiler_params=_params("arbitrary"),
        name="outproj",
    )(ohg_p, ohg_s, ofx_p, ofx_s, x, w_out, w_out, n_post, n_pre)


def _ffn_kernel(h_ref, x_ref, wg_ref, wu_ref, wd_ref, npost_ref, *rest, cast_tiles):
    n_cast = (len(rest) - 1) // 2
    o_ref = rest[n_cast]
    j = pl.program_id(1)

    if n_cast:
        @pl.when(pl.program_id(0) < cast_tiles)
        def _():
            for src, dst in zip(rest[:n_cast], rest[n_cast + 1:]):
                dst[...] = src[...].astype(BF16)

    @pl.when(j == 0)
    def _():
        o_ref[...] = jnp.zeros_like(o_ref)

    h = h_ref[...]
    a = jnp.dot(h, wg_ref[...], preferred_element_type=F32)
    b = jnp.dot(h, wu_ref[...], preferred_element_type=F32)
    sig_a, _, _ = _sigmoid_parts(a)
    o_ref[...] += jnp.dot((a * sig_a * b).astype(BF16), wd_ref[...], preferred_element_type=F32)

    @pl.when(j == pl.num_programs(1) - 1)
    def _():
        o_ref[...] = x_ref[...] + _rms(o_ref[...]) * npost_ref[...]


def _ffn(h2, x1, w_g, w_u, w_d, n_post, *, layer, tm, tf, cast_next=None):
    m, d = x1.shape
    f = w_g.shape[2]
    ni, nj = m // tm, f // tf
    in_specs = [pl.BlockSpec((tm, d), lambda i, j: (i, 0)),
                pl.BlockSpec((tm, d), lambda i, j: (i, 0)),
                pl.BlockSpec((None, d, tf), lambda i, j: (layer, 0, j)),
                pl.BlockSpec((None, d, tf), lambda i, j: (layer, 0, j)),
                pl.BlockSpec((None, tf, d), lambda i, j: (layer, j, 0)),
                pl.BlockSpec((1, d), lambda i, j: (0, 0))]
    args = [h2, x1, w_g, w_u, w_d, n_post]
    out_shape = [jax.ShapeDtypeStruct((m, d), F32)]
    out_specs = [pl.BlockSpec((tm, d), lambda i, j: (i, 0))]
    if cast_next is not None:
        assert ni >= CAST_SPLIT and d % (CAST_SPLIT * LANES) == 0
        *w32, nxt = cast_next
        ds = d // CAST_SPLIT
        ii = lambda i: jnp.minimum(i, CAST_SPLIT - 1)
        jj = lambda i, j: jnp.where(i < CAST_SPLIT, j, nj - 1)
        up_in = pl.BlockSpec((None, ds, tf), lambda i, j: (nxt, ii(i), jj(i, j)))
        down_in = pl.BlockSpec((None, tf, ds), lambda i, j: (nxt, jj(i, j), ii(i)))
        up_out = pl.BlockSpec((None, ds, tf), lambda i, j: (0, ii(i), jj(i, j)))
        down_out = pl.BlockSpec((None, tf, ds), lambda i, j: (0, jj(i, j), ii(i)))
        in_specs += [up_in, up_in, down_in]
        in_specs[0] = pl.BlockSpec((tm, d), lambda i, j: (i, 0), pipeline_mode=pl.Buffered(1))
        args += w32
        out_shape += [jax.ShapeDtypeStruct((1, d, f), BF16)] * 2 + [jax.ShapeDtypeStruct((1, f, d), BF16)]
        out_specs += [up_out, up_out, down_out]
    outs = pl.pallas_call(
        functools.partial(_ffn_kernel, cast_tiles=CAST_SPLIT),
        out_shape=tuple(out_shape),
        grid=(ni, nj),
        in_specs=in_specs,
        out_specs=tuple(out_specs),
        compiler_params=_params("arbitrary", "arbitrary"),
        name="ffn",
    )(*args)
    return outs[0], tuple(outs[1:])


def _tiles(batch, seq, nseq, steps, d_ff):
    m = batch * seq + nseq * steps
    tm = next(t for t in (1024, 512, 256, 128) if seq % t == 0 and m % t == 0)
    tm_out = next(t for t in (512, 256, 128) if (batch * seq) % t == 0 and (nseq * steps) % t == 0)
    tm_ffn = next(t for t in (768, 512, 256, 128) if m % t == 0)
    tf = next(t for t in (512, 256, 128) if d_ff % t == 0)
    tt = next(t for t in (512, 256, 128, HGRN_CHUNK) if seq % t == 0)
    tq = next(t for t in (1024, 512, 256, 128) if seq % t == 0)
    bb = next(b for b in (32, 16, 8, 4, 2, 1) if nseq % b == 0 and (batch * seq) % (b * steps) == 0)
    return dict(tm=tm, tn=1024, tm_out=tm_out, tm_ffn=tm_ffn, tf=tf, tt=tt, tq=tq, bb=bb, hp=4)


def kernel(x_prompt, x_sample, cache_k, cache_v, cache_logf, state_hgrn, page_table, w_in, b_fox_f,
           hg_lower_bounds, hg_norm_w, w_out, norm_mix_pre, norm_mix_post, norm_ffn_pre, norm_ffn_post,
           w_gate, w_up, w_down):
    batch, seq, d = x_prompt.shape
    nseq, steps, _ = x_sample.shape
    depth, n_pool, page, fh, fd = cache_k.shape
    hh, hk, hv = state_hgrn.shape[2:]
    d_ff = w_gate.shape[2]
    assert hk == LANES and hv == LANES and fd == LANES and hh == fh
    heads = hh
    n_main = 7 * heads * LANES
    assert w_in.shape[2] == n_main + fh and fh <= LANES
    mp = batch * seq
    t = _tiles(batch, seq, nseq, steps, d_ff)
    assert seq % 16 == 0

    x = jnp.concatenate([x_prompt.reshape(mp, d), x_sample.reshape(nseq * steps, d)], axis=0)
    row = lambda a: a.reshape(1, -1).astype(F32)
    cache_logf_t = cache_logf.transpose(0, 1, 3, 2)
    w_in_b, w_out_b = w_in.astype(BF16), w_out.astype(BF16)
    ffn_w = tuple(w[:1].astype(BF16) for w in (w_gate, w_up, w_down))

    outs = {n: [] for n in ("gp", "sp", "ks", "vs", "gs")}
    hgrn_sample_states = kv_prompt = None
    for l in range(depth):
        w_ff = jnp.pad(w_in[l, :, n_main:], ((0, 0), (0, LANES - fh))).astype(BF16)
        b_ff = jnp.pad(b_fox_f[l].astype(F32), (0, LANES - fh)).reshape(1, LANES)
        z, logf, c = _inproj(x, row(norm_mix_pre[l]), w_in_b, w_ff, b_ff, layer=l, n=n_main, tm=t["tm"],
                             tn=t["tn"], seq_len=seq)

        gn = row(hg_norm_w[l])
        lb_raw = hg_lower_bounds.astype(F32)
        ohg_p, s_p, kv_prompt = _hgrn_prompt(z, lb_raw, gn, kv_prompt, batch=batch, seq=seq, heads=heads,
                                             layer=l, chunk=HGRN_CHUNK, tt=t["tt"], hp=t["hp"])
        ohg_s, hgrn_sample_states = _hgrn_sample(z, lb_raw, gn, state_hgrn, hgrn_sample_states, row0=mp,
                                                 nseq=nseq, steps=steps, heads=heads, layer=l, bb=t["bb"])

        ofx_p = _fox_prompt(z, c, batch=batch, seq=seq, heads=heads, tq=t["tq"])
        ofx_s = _fox_decode(z, logf, cache_k, cache_v, cache_logf_t, page_table, row0=mp, layer=l, heads=heads,
                            steps=steps)

        hw = heads * LANES
        x1, h2 = _outproj(ohg_p, ohg_s, ofx_p, ofx_s, x, w_out_b, row(norm_mix_post[l]), row(norm_ffn_pre[l]),
                          layer=l, tm=t["tm_out"])
        cast_next = (w_gate, w_up, w_down, l + 1) if l + 1 < depth else None
        x, ffn_w = _ffn(h2, x1, *ffn_w, row(norm_ffn_post[l]), layer=0, tm=t["tm_ffn"], tf=t["tf"],
                        cast_next=cast_next)

        kcol, vcol = 5 * heads * LANES, 6 * heads * LANES
        outs["gp"].append(logf[:mp, :fh].reshape(batch, seq, fh))
        outs["sp"].append(s_p)
        outs["ks"].append(z[mp:, kcol:kcol + hw].reshape(nseq, steps, fh, fd))
        outs["vs"].append(z[mp:, vcol:vcol + hw].reshape(nseq, steps, fh, fd))
        outs["gs"].append(logf[mp:, :fh].reshape(nseq, steps, fh))

    st = {n: jnp.stack(v) for n, v in outs.items()}
    k_prompt, v_prompt = (a.reshape(depth, batch, seq, fh, fd) for a in kv_prompt)
    return (x[:mp].reshape(batch, seq, d), x[mp:].reshape(nseq, steps, d),
            k_prompt, v_prompt, st["gp"], st["sp"], st["ks"], st["vs"], st["gs"], hgrn_sample_states)
```
